```python
import math
import jax, jax.numpy as jnp
from jax import lax
import numpy as np

D_MODEL = 1024
BATCH = 1
SEQ = 16384
DEPTH = 1

D_MIX = D_MODEL
RET_WIDTH = D_MIX // 2
RET_HEADS = 8
RET_HEAD_DIM = RET_WIDTH // RET_HEADS
CONV_WIDTH = D_MIX - RET_WIDTH
CONV_GROUPS = 8
CONV_GROUP_DIM = CONV_WIDTH // CONV_GROUPS
SHORT_CONV_K = 3
CHUNK = 128
D_FF = 2816
FFN_CONV_K = 3
ROPE_BASE = 10000.0
EPS = 1e-6
N_MOD = 6
IN_COLS = 4 * RET_WIDTH + 3 * CONV_WIDTH

kernel_name = "hymba_retention_shortconv_convglu_adaln"


def rms_norm(x, g):
    xf = x.astype(jnp.float32)
    y = xf * lax.rsqrt(jnp.mean(xf * xf, axis=-1, keepdims=True) + EPS)
    return (y * g.astype(jnp.float32)).astype(x.dtype)


def group_rms_norm(x, g, n_groups):
    b, s, w = x.shape
    xf = x.astype(jnp.float32).reshape(b, s, n_groups, w // n_groups)
    xf = xf * lax.rsqrt(jnp.mean(xf * xf, axis=-1, keepdims=True) + EPS)
    return (xf.reshape(b, s, w) * g.astype(jnp.float32)).astype(x.dtype)


def modulate(h, shift, scale):
    return h * (1.0 + scale[:, None, :]) + shift[:, None, :]


def causal_dwconv(x, w):
    k = w.shape[0]
    s = x.shape[1]
    xp = jnp.pad(x, ((0, 0), (k - 1, 0), (0, 0)))
    y = xp[:, 0:s, :] * w[0]
    for j in range(1, k):
        y = y + xp[:, j:j + s, :] * w[j]
    return y


def rotary(x, positions):
    half = x.shape[-1] // 2
    inv_freq = ROPE_BASE ** (-jnp.arange(half, dtype=jnp.float32) / half)
    ang = positions.astype(jnp.float32)[..., None] * inv_freq
    cos = jnp.cos(ang)[:, :, None, :]
    sin = jnp.sin(ang)[:, :, None, :]
    xf = x.astype(jnp.float32)
    x1, x2 = xf[..., :half], xf[..., half:]
    return jnp.concatenate([x1 * cos - x2 * sin, x1 * sin + x2 * cos], axis=-1)


def retention_chunkwise(q, k, v):
    b, s, h, dh = q.shape
    nc = s // CHUNK
    log_gamma = jnp.log1p(-jnp.exp2(-5.0 - jnp.arange(h, dtype=jnp.float32)))
    q = q.reshape(b, nc, CHUNK, h, dh)
    k = k.reshape(b, nc, CHUNK, h, dh) * (dh ** -0.5)
    v = v.reshape(b, nc, CHUNK, h, dh)
    idx = jnp.arange(CHUNK, dtype=jnp.float32)
    rel = idx[:, None] - idx[None, :]
    decay_mask = jnp.where(rel[None] >= 0.0,
                           jnp.exp(jnp.maximum(rel, 0.0)[None] * log_gamma[:, None, None]),
                           0.0)
    scores = jnp.einsum('bnqhd,bnkhd->bnhqk', q, k) * decay_mask[None, None]
    y_inner = jnp.einsum('bnhqk,bnkhd->bnqhd', scores, v)
    zeta = jnp.exp((CHUNK - 1.0 - idx)[None, :] * log_gamma[:, None])
    kv_chunk = jnp.einsum('bnkhd,hk,bnkhe->bnhde', k, zeta, v)
    chunk_decay = jnp.exp(CHUNK * log_gamma)[None, :, None, None]

    def step(state, kv):
        return state * chunk_decay + kv, state

    init = jnp.zeros((b, h, dh, dh), dtype=jnp.float32)
    _, r_prev = lax.scan(step, init, jnp.moveaxis(kv_chunk, 1, 0))
    r_prev = jnp.moveaxis(r_prev, 0, 1)
    xi = jnp.exp((idx + 1.0)[None, :] * log_gamma[:, None])
    y_cross = jnp.einsum('bnqhd,bnhde,hq->bnqhe', q, r_prev, xi)
    return (y_inner + y_cross).reshape(b, s, h, dh)


def setup_inputs(seed: int = 0) -> dict:
    key = jax.random.key(seed)
    ks = jax.random.split(key, 20)
    f32 = jnp.float32
    d = D_MODEL

    def nrm(k, shape, scale):
        return jax.random.normal(k, shape, dtype=f32) * scale

    x = nrm(ks[0], (BATCH, SEQ, d), 1.0)
    c = nrm(ks[1], (BATCH, d), 1.0)
    positions = jnp.broadcast_to(jnp.arange(SEQ, dtype=jnp.int32)[None, :], (BATCH, SEQ))
    return {
        "x": x,
        "c": c,
        "positions": positions,
        "mod_w": nrm(ks[2], (DEPTH, d, N_MOD * d), 0.5 * d ** -0.5),
        "mod_b": nrm(ks[3], (DEPTH, N_MOD * d), 0.01),
        "norm1_g": 1.0 + nrm(ks[4], (DEPTH, d), 0.02),
        "w_in": nrm(ks[5], (DEPTH, d, IN_COLS), d ** -0.5),
        "ret_norm_g": 1.0 + nrm(ks[6], (DEPTH, RET_WIDTH), 0.02),
        "short_conv_w": nrm(ks[7], (DEPTH, SHORT_CONV_K, CONV_WIDTH), SHORT_CONV_K ** -0.5),
        "conv_norm_g": 1.0 + nrm(ks[8], (DEPTH, CONV_WIDTH), 0.02),
        "w_out": nrm(ks[9], (DEPTH, D_MIX, d), D_MIX ** -0.5),
        "norm2_g": 1.0 + nrm(ks[10], (DEPTH, d), 0.02),
        "w_up": nrm(ks[11], (DEPTH, d, 2 * D_FF), d ** -0.5),
        "ffn_conv_w": nrm(ks[12], (DEPTH, FFN_CONV_K, D_FF), FFN_CONV_K ** -0.5),
        "ffn_conv_b": nrm(ks[13], (DEPTH, D_FF), 0.01),
        "w_down": nrm(ks[14], (DEPTH, D_FF, d), D_FF ** -0.5),
        "final_mod_w": nrm(ks[15], (d, 2 * d), 0.5 * d ** -0.5),
        "final_mod_b": nrm(ks[16], (2 * d,), 0.01),
        "final_norm_g": 1.0 + nrm(ks[17], (d,), 0.02),
    }


def reference(x, c, positions, mod_w, mod_b, norm1_g, w_in, ret_norm_g, short_conv_w,
              conv_norm_g, w_out, norm2_g, w_up, ffn_conv_w, ffn_conv_b, w_down,
              final_mod_w, final_mod_b, final_norm_g):
    b, s, d = x.shape
    c_act = jax.nn.silu(c)
    for layer in range(DEPTH):
        mod = (c_act @ mod_w[layer] + mod_b[layer]).reshape(b, N_MOD, d)
        shift1, scale1, gate1 = mod[:, 0], mod[:, 1], mod[:, 2]
        shift2, scale2, gate2 = mod[:, 3], mod[:, 4], mod[:, 5]

        h = modulate(rms_norm(x, norm1_g[layer]), shift1, scale1)
        proj = h @ w_in[layer]
        q, k, v, g, cb, cc, cx = jnp.split(
            proj,
            [RET_WIDTH, 2 * RET_WIDTH, 3 * RET_WIDTH, 4 * RET_WIDTH,
             4 * RET_WIDTH + CONV_WIDTH, 4 * RET_WIDTH + 2 * CONV_WIDTH],
            axis=-1)
        q = rotary(q.reshape(b, s, RET_HEADS, RET_HEAD_DIM), positions)
        k = rotary(k.reshape(b, s, RET_HEADS, RET_HEAD_DIM), positions)
        v = v.reshape(b, s, RET_HEADS, RET_HEAD_DIM).astype(jnp.float32)
        y_ret = retention_chunkwise(q, k, v).reshape(b, s, RET_WIDTH).astype(x.dtype)
        y_ret = group_rms_norm(y_ret, ret_norm_g[layer], RET_HEADS) * jax.nn.silu(g)
        y_conv = cb * causal_dwconv(cc * cx, short_conv_w[layer])
        y_conv = group_rms_norm(y_conv, conv_norm_g[layer], CONV_GROUPS)
        mixed = jnp.concatenate([y_ret, y_conv], axis=-1) @ w_out[layer]
        x = x + gate1[:, None, :] * mixed

        h = modulate(rms_norm(x, norm2_g[layer]), shift2, scale2)
        up = h @ w_up[layer]
        a, val = up[..., :D_FF], up[..., D_FF:]
        a = causal_dwconv(a, ffn_conv_w[layer]) + ffn_conv_b[layer]
        ffn = (jax.nn.silu(a) * val) @ w_down[layer]
        x = x + gate2[:, None, :] * ffn

    fmod = (c_act @ final_mod_w + final_mod_b).reshape(b, 2, d)
    return modulate(rms_norm(x, final_norm_g), fmod[:, 0], fmod[:, 1])
```

```python
import functools

import numpy as np
import jax
import jax.numpy as jnp
from jax import lax
from jax.experimental import pallas as pl
from jax.experimental.pallas import tpu as pltpu

F32 = jnp.float32
BF16 = jnp.bfloat16

D_MODEL = 1024
RET_WIDTH = 512
RET_HEADS = 8
HEAD_DIM = 64
CONV_WIDTH = 512
GROUP_DIM = 64
CHUNK = 128
D_FF = 2816
ROPE_BASE = 10000.0
EPS = 1e-6
IN_COLS = 4 * RET_WIDTH + 3 * CONV_WIDTH

LANES = 128
SUBLANES = 8
MXU_DIM = 256
HEAD_PAIRS = RET_WIDTH // LANES
VMEM_LIMIT_BYTES = 56 * 1024 * 1024

TS_MIX = 512
TS_FFN = 512


def _const_spec(shape):
    return pl.BlockSpec(shape, lambda i: (0,) * len(shape), pipeline_mode=pl.Buffered(1))


def _rms(x, gain):
    ms = jnp.mean(x * x, axis=-1, keepdims=True)
    return x * lax.rsqrt(ms + EPS) * gain


def _silu(x):
    return x * jax.nn.sigmoid(x)


def _mod_kernel(c_ref, w_ref, b_ref, o_ref):
    c = c_ref[...]
    o_ref[...] = jnp.sum(_silu(c) * w_ref[...], axis=0, keepdims=True) + b_ref[...]


def _mod_call(c_col, w, b_row):
    d, n = w.shape
    bn = 1024
    return pl.pallas_call(
        _mod_kernel,
        out_shape=jax.ShapeDtypeStruct((1, n), F32),
        grid=(n // bn,),
        in_specs=[
            pl.BlockSpec((d, 1), lambda j: (0, 0)),
            pl.BlockSpec((d, bn), lambda j: (0, j)),
            pl.BlockSpec((1, bn), lambda j: (0, j)),
        ],
        out_specs=pl.BlockSpec((1, bn), lambda j: (0, j)),
        compiler_params=pltpu.CompilerParams(
            dimension_semantics=("arbitrary",), vmem_limit_bytes=VMEM_LIMIT_BYTES),
        name="adaln_mod",
    )(c_col, w, b_row)


def _group_sumsq(z, gmat_ref):
    outs = []
    for lo in range(0, z.shape[1], MXU_DIM):
        zz = z[:, lo:lo + MXU_DIM]
        hi = zz.astype(BF16)
        rem = (zz - hi.astype(F32)).astype(BF16)
        g = gmat_ref[...]
        outs.append(jnp.dot(hi, g, preferred_element_type=F32)
                    + jnp.dot(rem, g, preferred_element_type=F32))
    return jnp.concatenate(outs, axis=1)


def _group_rms(y, gain, gmat_ref):
    ss = _group_sumsq(y * y, gmat_ref)
    return y * lax.rsqrt(ss * (1.0 / GROUP_DIM) + EPS) * gain


def _causal_conv3(tail_ref, w_ref, ts):
    w = w_ref[...]
    y = (tail_ref[SUBLANES - 2:ts + SUBLANES - 2, :] * w[0:1, :]
         + tail_ref[SUBLANES - 1:ts + SUBLANES - 1, :] * w[1:2, :]
         + tail_ref[SUBLANES:ts + SUBLANES, :] * w[2:3, :])
    tail_ref[0:SUBLANES, :] = tail_ref[ts:ts + SUBLANES, :]
    return y


def _mix_kernel(x_ref, pos_ref, mod_ref, g1_ref, win_ref, invf_ref, dmask_ref, xi_ref, zeta_ref,
                rdecay_ref, rmask_ref, retg_ref, scw_ref, cng_ref, gmat_ref, wout_ref,
                o_ref, q_s, kt_s, v_s, y_s, u_s, r_s):
    ts = x_ref.shape[0]

    @pl.when(pl.program_id(0) == 0)
    def _():
        r_s[...] = jnp.zeros_like(r_s)
        u_s[0:SUBLANES, :] = jnp.zeros((SUBLANES, CONV_WIDTH), F32)

    x = x_ref[...]
    h = (_rms(x, g1_ref[...]) * (1.0 + mod_ref[1:2, :]) + mod_ref[0:1, :]).astype(BF16)

    def proj(col):
        return jnp.dot(h, win_ref[:, col:col + RET_WIDTH], preferred_element_type=F32)

    ang = pos_ref[...].astype(F32) * invf_ref[...]
    cos = jnp.cos(ang)
    sin = jnp.sin(ang)
    lane = lax.broadcasted_iota(jnp.int32, (ts, LANES), 1)
    first_half = (lane & (HEAD_DIM // 2)) == 0
    sin_signed = jnp.where(first_half, -sin, sin)

    def rotary(t):
        swapped = jnp.where(first_half, pltpu.roll(t, LANES - HEAD_DIM // 2, 1),
                            pltpu.roll(t, HEAD_DIM // 2, 1))
        return t * cos + swapped * sin_signed

    q = proj(0)
    k = proj(RET_WIDTH)
    k_rot = []
    for p in range(HEAD_PAIRS):
        ln = slice(p * LANES, (p + 1) * LANES)
        q_s[:, ln] = rotary(q[:, ln]).astype(BF16)
        k_rot.append(rotary(k[:, ln]) * (HEAD_DIM ** -0.5))
    kt_s[...] = jnp.concatenate(k_rot, axis=1).T.astype(BF16)
    v_s[...] = proj(2 * RET_WIDTH).astype(BF16)

    lane_c = lax.broadcasted_iota(jnp.int32, (CHUNK, LANES), 1)
    head0 = lane_c < HEAD_DIM
    for c in range(ts // CHUNK):
        rows = slice(c * CHUNK, (c + 1) * CHUNK)
        for p in range(HEAD_PAIRS):
            ln = slice(p * LANES, (p + 1) * LANES)
            qc = q_s[rows, ln]
            ktc = kt_s[ln, rows]
            vc = v_s[rows, ln]
            zero = jnp.zeros_like(qc)
            q2 = jnp.concatenate([jnp.where(head0, qc, zero), jnp.where(head0, zero, qc)], axis=0)
            scores = jnp.dot(q2, ktc, preferred_element_type=F32) * dmask_ref[p]
            pm = scores.astype(BF16)
            pcat = jnp.concatenate([pm[:CHUNK], pm[CHUNK:]], axis=1)
            vcat = jnp.concatenate([jnp.where(head0, vc, zero), jnp.where(head0, zero, vc)], axis=0)
            y_inner = jnp.dot(pcat, vcat, preferred_element_type=F32)
            r = r_s[p]
            y_cross = jnp.dot(qc, r.astype(BF16), preferred_element_type=F32) * xi_ref[p]
            y_s[rows, ln] = y_inner + y_cross
            vz = (vc.astype(F32) * zeta_ref[p]).astype(BF16)
            kv = jnp.dot(ktc, vz, preferred_element_type=F32)
            r_s[p] = r * rdecay_ref[p] + kv * rmask_ref[p]

    y_ret = _group_rms(y_s[...], retg_ref[...], gmat_ref) * _silu(proj(3 * RET_WIDTH))

    u_s[SUBLANES:ts + SUBLANES, :] = proj(4 * RET_WIDTH + CONV_WIDTH) * proj(4 * RET_WIDTH + 2 * CONV_WIDTH)
    y_conv = proj(4 * RET_WIDTH) * _causal_conv3(u_s, scw_ref, ts)
    y_conv = _group_rms(y_conv, cng_ref[...], gmat_ref)

    cat = jnp.concatenate([y_ret.astype(BF16), y_conv.astype(BF16)], axis=1)
    mixed = jnp.dot(cat, wout_ref[...], preferred_element_type=F32)
    o_ref[...] = x + mod_ref[2:3, :] * mixed


def _retention_tables():
    h = jnp.arange(RET_HEADS, dtype=F32)
    log_gamma = jnp.log1p(-jnp.exp2(-5.0 - h))
    idx = jnp.arange(CHUNK, dtype=F32)
    rel = idx[:, None] - idx[None, :]
    dmask = jnp.where(rel[None] >= 0.0,
                      jnp.exp(jnp.maximum(rel, 0.0)[None] * log_gamma[:, None, None]), 0.0)
    dmask = dmask.reshape(HEAD_PAIRS, 2 * CHUNK, CHUNK)
    zeta = jnp.exp((CHUNK - 1.0 - idx)[None, :] * log_gamma[:, None])
    xi = jnp.exp((idx + 1.0)[None, :] * log_gamma[:, None])
    chunk_decay = jnp.exp(CHUNK * log_gamma)

    def per_lane(t):
        t = t.reshape(HEAD_PAIRS, 2, CHUNK)
        return jnp.repeat(jnp.transpose(t, (0, 2, 1)), HEAD_DIM, axis=2)

    lane_head = np.arange(LANES) // HEAD_DIM
    rmask = jnp.asarray((lane_head[:, None] == lane_head[None, :]).astype(np.float32))
    rmask = jnp.broadcast_to(rmask, (HEAD_PAIRS, LANES, LANES))
    rdecay = jnp.repeat(chunk_decay.reshape(HEAD_PAIRS, 1, 2), HEAD_DIM, axis=2)
    rdecay = jnp.broadcast_to(rdecay, (HEAD_PAIRS, LANES, LANES))
    return dmask, per_lane(xi), per_lane(zeta), rdecay, rmask


def _group_matrix():
    g = np.arange(MXU_DIM) // GROUP_DIM
    return jnp.asarray((g[:, None] == g[None, :]).astype(np.float32)).astype(BF16)


def _mix_call(x, pos_col, mod, g1, w_in, retg, scw, cng, w_out):
    s = x.shape[0]
    ts = TS_MIX
    half = HEAD_DIM // 2
    inv_freq = ROPE_BASE ** (-jnp.arange(half, dtype=F32) / half)
    invf = jnp.tile(inv_freq, LANES // half)[None, :]
    dmask, xi, zeta, rdecay, rmask = _retention_tables()
    row = lambda i: (i, 0)
    return pl.pallas_call(
        _mix_kernel,
        out_shape=jax.ShapeDtypeStruct((s, D_MODEL), F32),
        grid=(s // ts,),
        in_specs=[
            pl.BlockSpec((ts, D_MODEL), row),
            pl.BlockSpec((ts, 1), row),
            _const_spec(mod.shape),
            _const_spec(g1.shape),
            _const_spec(w_in.shape),
            _const_spec(invf.shape),
            _const_spec(dmask.shape),
            _const_spec(xi.shape),
            _const_spec(zeta.shape),
            _const_spec(rdecay.shape),
            _const_spec(rmask.shape),
            _const_spec(retg.shape),
            _const_spec(scw.shape),
            _const_spec(cng.shape),
            _const_spec((MXU_DIM, MXU_DIM)),
            _const_spec(w_out.shape),
        ],
        out_specs=pl.BlockSpec((ts, D_MODEL), row),
        scratch_shapes=[
            pltpu.VMEM((ts, RET_WIDTH), BF16),
            pltpu.VMEM((RET_WIDTH, ts), BF16),
            pltpu.VMEM((ts, RET_WIDTH), BF16),
            pltpu.VMEM((ts, RET_WIDTH), F32),
            pltpu.VMEM((ts + SUBLANES, CONV_WIDTH), F32),
            pltpu.VMEM((HEAD_PAIRS, LANES, LANES), F32),
        ],
        compiler_params=pltpu.CompilerParams(
            dimension_semantics=("arbitrary",), vmem_limit_bytes=VMEM_LIMIT_BYTES),
        name="token_mix",
    )(x, pos_col, mod, g1, w_in, invf, dmask, xi, zeta, rdecay, rmask, retg, scw, cng,
      _group_matrix(), w_out)


def _ffn_kernel(x_ref, mod_ref, fmod_ref, g2_ref, wup_ref, fcw_ref, fcb_ref, wdown_ref, fg_ref,
                o_ref, a_s):
    ts = x_ref.shape[0]

    @pl.when(pl.program_id(0) == 0)
    def _():
        a_s[0:SUBLANES, :] = jnp.zeros((SUBLANES, D_FF), F32)

    x = x_ref[...]
    h = (_rms(x, g2_ref[...]) * (1.0 + mod_ref[4:5, :]) + mod_ref[3:4, :]).astype(BF16)
    a_s[SUBLANES:ts + SUBLANES, :] = jnp.dot(h, wup_ref[:, 0:D_FF], preferred_element_type=F32)
    val = jnp.dot(h, wup_ref[:, D_FF:2 * D_FF], preferred_element_type=F32)
    a = _causal_conv3(a_s, fcw_ref, ts) + fcb_ref[...]
    act = (_silu(a) * val).astype(BF16)
    ffn = jnp.dot(act, wdown_ref[...], preferred_element_type=F32)
    x2 = x + mod_ref[5:6, :] * ffn
    o_ref[...] = _rms(x2, fg_ref[...]) * (1.0 + fmod_ref[1:2, :]) + fmod_ref[0:1, :]


def _ffn_call(x, mod, fmod, g2, w_up, fcw, fcb, w_down, fg):
    s = x.shape[0]
    ts = TS_FFN
    row = lambda i: (i, 0)
    return pl.pallas_call(
        _ffn_kernel,
        out_shape=jax.ShapeDtypeStruct((s, D_MODEL), F32),
        grid=(s // ts,),
        in_specs=[
            pl.BlockSpec((ts, D_MODEL), row),
            _const_spec(mod.shape),
            _const_spec(fmod.shape),
            _const_spec(g2.shape),
            _const_spec(w_up.shape),
            _const_spec(fcw.shape),
            _const_spec(fcb.shape),
            _const_spec(w_down.shape),
            _const_spec(fg.shape),
        ],
        out_specs=pl.BlockSpec((ts, D_MODEL), row),
        scratch_shapes=[pltpu.VMEM((ts + SUBLANES, D_FF), F32)],
        compiler_params=pltpu.CompilerParams(
            dimension_semantics=("arbitrary",), vmem_limit_bytes=VMEM_LIMIT_BYTES),
        name="convglu_ffn",
    )(x, mod, fmod, g2, w_up, fcw, fcb, w_down, fg)


def kernel(x, c, positions, mod_w, mod_b, norm1_g, w_in, ret_norm_g, short_conv_w, conv_norm_g,
           w_out, norm2_g, w_up, ffn_conv_w, ffn_conv_b, w_down, final_mod_w, final_mod_b,
           final_norm_g):
    b, s, d = x.shape
    assert (b, d) == (1, D_MODEL) and mod_w.shape[0] == 1 and s % max(TS_MIX, TS_FFN) == 0
    c_col = c.reshape(d, 1)
    mod = _mod_call(c_col, mod_w[0], mod_b).reshape(6, d)
    fmod = _mod_call(c_col, final_mod_w, final_mod_b[None, :]).reshape(2, d)

    x2d = x.reshape(s, d)
    x1 = _mix_call(x2d, positions.reshape(s, 1), mod, norm1_g, w_in[0].astype(BF16), ret_norm_g,
                   short_conv_w[0], conv_norm_g, w_out[0].astype(BF16))
    out = _ffn_call(x1, mod, fmod, norm2_g, w_up[0].astype(BF16), ffn_conv_w[0], ffn_conv_b,
                    w_down[0].astype(BF16), final_norm_g[None, :])
    return out.reshape(b, s, d)
```

```python
import functools

import numpy as np
import jax
import jax.numpy as jnp
from jax import lax
from jax.experimental import pallas as pl
from jax.experimental.pallas import tpu as pltpu

F32 = jnp.float32
BF16 = jnp.bfloat16

D_MODEL = 1024
RET_WIDTH = 512
RET_HEADS = 8
HEAD_DIM = 64
CONV_WIDTH = 512
GROUP_DIM = 64
CHUNK = 128
D_FF = 2816
ROPE_BASE = 10000.0
EPS = 1e-6
IN_COLS = 4 * RET_WIDTH + 3 * CONV_WIDTH

LANES = 128
SUBLANES = 8
MXU_DIM = 256
HEAD_PAIRS = RET_WIDTH // LANES
VMEM_LIMIT_BYTES = 56 * 1024 * 1024

TS_MIX = 512
TS_FFN = 512


def _const_spec(shape):
    return pl.BlockSpec(shape, lambda i: (0,) * len(shape), pipeline_mode=pl.Buffered(1))


def _rms(x, gain):
    ms = jnp.mean(x * x, axis=-1, keepdims=True)
    return x * lax.rsqrt(ms + EPS) * gain


def _silu(x):
    return x * jax.nn.sigmoid(x)


MOD_BLOCK = 1024


def _mod_kernel(c_ref, w1_ref, w2_ref, b_ref, o_ref, *, n1_blocks):
    ca = _silu(c_ref[...])

    @pl.when(pl.program_id(0) < n1_blocks)
    def _():
        o_ref[...] = jnp.sum(ca * w1_ref[...], axis=0, keepdims=True) + b_ref[...]

    @pl.when(pl.program_id(0) >= n1_blocks)
    def _():
        o_ref[...] = jnp.sum(ca * w2_ref[...], axis=0, keepdims=True) + b_ref[...]


def _mod_call(c_col, w1, w2, b_row):
    d, n1 = w1.shape
    n2 = w2.shape[1]
    bn = MOD_BLOCK
    nb1, nb2 = n1 // bn, n2 // bn
    return pl.pallas_call(
        functools.partial(_mod_kernel, n1_blocks=nb1),
        out_shape=jax.ShapeDtypeStruct((1, n1 + n2), F32),
        grid=(nb1 + nb2,),
        in_specs=[
            pl.BlockSpec((d, 1), lambda j: (0, 0)),
            pl.BlockSpec((d, bn), lambda j: (0, jnp.minimum(j, nb1 - 1))),
            pl.BlockSpec((d, bn), lambda j: (0, jnp.maximum(j - nb1, 0))),
            pl.BlockSpec((1, bn), lambda j: (0, j)),
        ],
        out_specs=pl.BlockSpec((1, bn), lambda j: (0, j)),
        compiler_params=pltpu.CompilerParams(
            dimension_semantics=("arbitrary",), vmem_limit_bytes=VMEM_LIMIT_BYTES),
        name="adaln_mod",
    )(c_col, w1, w2, b_row)


def _group_sumsq(z, gmat_ref):
    outs = []
    for lo in range(0, z.shape[1], MXU_DIM):
        zz = z[:, lo:lo + MXU_DIM]
        hi = zz.astype(BF16)
        rem = (zz - hi.astype(F32)).astype(BF16)
        g = gmat_ref[...]
        outs.append(jnp.dot(hi, g, preferred_element_type=F32)
                    + jnp.dot(rem, g, preferred_element_type=F32))
    return jnp.concatenate(outs, axis=1)


def _group_rms(y, gain, gmat_ref):
    ss = _group_sumsq(y * y, gmat_ref)
    return y * lax.rsqrt(ss * (1.0 / GROUP_DIM) + EPS) * gain


def _causal_conv3(tail_ref, w_ref, ts):
    w = w_ref[...]
    y = (tail_ref[SUBLANES - 2:ts + SUBLANES - 2, :] * w[0:1, :]
         + tail_ref[SUBLANES - 1:ts + SUBLANES - 1, :] * w[1:2, :]
         + tail_ref[SUBLANES:ts + SUBLANES, :] * w[2:3, :])
    tail_ref[0:SUBLANES, :] = tail_ref[ts:ts + SUBLANES, :]
    return y


def _mix_kernel(x_ref, pos_ref, mod_ref, g1_ref, win_ref, invf_ref, dmask_ref, xi_ref, zeta_ref,
                rdecay_ref, rmask_ref, retg_ref, scw_ref, cng_ref, gmat_ref, wout_ref,
                o_ref, q_s, kt_s, v_s, y_s, u_s, r_s):
    ts = x_ref.shape[0]

    @pl.when(pl.program_id(0) == 0)
    def _():
        r_s[...] = jnp.zeros_like(r_s)
        u_s[0:SUBLANES, :] = jnp.zeros((SUBLANES, CONV_WIDTH), F32)

    x = x_ref[...]
    h = (_rms(x, g1_ref[...]) * (1.0 + mod_ref[1:2, :]) + mod_ref[0:1, :]).astype(BF16)

    def proj(col):
        return jnp.dot(h, win_ref[:, col:col + RET_WIDTH], preferred_element_type=F32)

    half = HEAD_DIM // 2
    ang_t = invf_ref[...] * pos_ref[...].astype(F32)
    cos_t = jnp.cos(ang_t)
    sin_t = jnp.sin(ang_t)
    trig = jnp.concatenate([cos_t, sin_t, cos_t, sin_t], axis=0).T
    lane = lax.broadcasted_iota(jnp.int32, (ts, LANES), 1)
    first_half = (lane & half) == 0
    cos = jnp.where(first_half, trig, pltpu.roll(trig, half, 1))
    sin_signed = jnp.where(first_half, -pltpu.roll(trig, LANES - half, 1), trig)

    def rotary(t):
        swapped = jnp.where(first_half, pltpu.roll(t, LANES - half, 1), pltpu.roll(t, half, 1))
        return t * cos + swapped * sin_signed

    q = proj(0)
    k = proj(RET_WIDTH)
    k_rot = []
    for p in range(HEAD_PAIRS):
        ln = slice(p * LANES, (p + 1) * LANES)
        q_s[:, ln] = rotary(q[:, ln]).astype(BF16)
        k_rot.append(rotary(k[:, ln]) * (HEAD_DIM ** -0.5))
    k_t = jnp.concatenate(k_rot, axis=1).T
    dim_head0 = (lax.broadcasted_iota(jnp.int32, k_t.shape, 0) & HEAD_DIM) == 0
    kt_s[0] = k_t.astype(BF16)
    kt_s[1] = jnp.where(dim_head0, k_t, 0.0).astype(BF16)
    kt_s[2] = jnp.where(dim_head0, 0.0, k_t).astype(BF16)
    v = proj(2 * RET_WIDTH)
    lane_head0 = (lax.broadcasted_iota(jnp.int32, v.shape, 1) & HEAD_DIM) == 0
    v_s[0] = v.astype(BF16)
    v_s[1] = jnp.where(lane_head0, v, 0.0).astype(BF16)
    v_s[2] = jnp.where(lane_head0, 0.0, v).astype(BF16)

    for c in range(ts // CHUNK):
        rows = slice(c * CHUNK, (c + 1) * CHUNK)
        for p in range(HEAD_PAIRS):
            ln = slice(p * LANES, (p + 1) * LANES)
            qc = q_s[rows, ln]
            kcat = jnp.concatenate([kt_s[1, ln, rows], kt_s[2, ln, rows]], axis=1)
            scores = jnp.dot(qc, kcat, preferred_element_type=F32) * dmask_ref[p]
            vcat = jnp.concatenate([v_s[1, rows, ln], v_s[2, rows, ln]], axis=0)
            y_inner = jnp.dot(scores.astype(BF16), vcat, preferred_element_type=F32)
            r = r_s[p]
            y_cross = jnp.dot(qc, r.astype(BF16), preferred_element_type=F32) * xi_ref[p]
            y_s[rows, ln] = y_inner + y_cross
            vz = (v_s[0, rows, ln].astype(F32) * zeta_ref[p]).astype(BF16)
            kv = jnp.dot(kt_s[0, ln, rows], vz, preferred_element_type=F32)
            r_s[p] = r * rdecay_ref[p] + kv * rmask_ref[p]

    y_ret = _group_rms(y_s[...], retg_ref[...], gmat_ref) * _silu(proj(3 * RET_WIDTH))

    u_s[SUBLANES:ts + SUBLANES, :] = proj(4 * RET_WIDTH + CONV_WIDTH) * proj(4 * RET_WIDTH + 2 * CONV_WIDTH)
    y_conv = proj(4 * RET_WIDTH) * _causal_conv3(u_s, scw_ref, ts)
    y_conv = _group_rms(y_conv, cng_ref[...], gmat_ref)

    cat = jnp.concatenate([y_ret.astype(BF16), y_conv.astype(BF16)], axis=1)
    mixed = jnp.dot(cat, wout_ref[...], preferred_element_type=F32)
    o_ref[...] = x + mod_ref[2:3, :] * mixed


def _retention_tables():
    h = np.arange(RET_HEADS, dtype=np.float64)
    log_gamma = np.log1p(-np.exp2(-5.0 - h))
    idx = np.arange(CHUNK, dtype=np.float64)
    rel = idx[:, None] - idx[None, :]
    dmask = np.where(rel[None] >= 0.0,
                     np.exp(np.maximum(rel, 0.0)[None] * log_gamma[:, None, None]), 0.0)
    dmask = dmask.reshape(HEAD_PAIRS, 2, CHUNK, CHUNK).transpose(0, 2, 1, 3)
    dmask = dmask.reshape(HEAD_PAIRS, CHUNK, 2 * CHUNK)
    zeta = np.exp((CHUNK - 1.0 - idx)[None, :] * log_gamma[:, None])
    xi = np.exp((idx + 1.0)[None, :] * log_gamma[:, None])
    chunk_decay = np.exp(CHUNK * log_gamma)

    def per_lane(t):
        t = t.reshape(HEAD_PAIRS, 2, CHUNK)
        return np.repeat(np.transpose(t, (0, 2, 1)), HEAD_DIM, axis=2)

    lane_head = np.arange(LANES) // HEAD_DIM
    rmask = np.broadcast_to(lane_head[:, None] == lane_head[None, :], (HEAD_PAIRS, LANES, LANES))
    rdecay = np.repeat(chunk_decay.reshape(HEAD_PAIRS, 1, 2), HEAD_DIM, axis=2)
    rdecay = np.broadcast_to(rdecay, (HEAD_PAIRS, LANES, LANES))
    return tuple(jnp.asarray(np.ascontiguousarray(t), dtype=F32)
                 for t in (dmask, per_lane(xi), per_lane(zeta), rdecay, rmask))


def _group_matrix():
    g = np.arange(MXU_DIM) // GROUP_DIM
    return jnp.asarray(g[:, None] == g[None, :], dtype=BF16)


def _mix_call(x, pos_row, mod, g1, w_in, retg, scw, cng, w_out):
    s = x.shape[0]
    ts = TS_MIX
    half = HEAD_DIM // 2
    invf = (ROPE_BASE ** (-jnp.arange(half, dtype=F32) / half))[:, None]
    dmask, xi, zeta, rdecay, rmask = _retention_tables()
    row = lambda i: (i, 0)
    return pl.pallas_call(
        _mix_kernel,
        out_shape=jax.ShapeDtypeStruct((s, D_MODEL), F32),
        grid=(s // ts,),
        in_specs=[
            pl.BlockSpec((ts, D_MODEL), row),
            pl.BlockSpec((1, ts), lambda i: (0, i)),
            _const_spec(mod.shape),
            _const_spec(g1.shape),
            _const_spec(w_in.shape),
            _const_spec(invf.shape),
            _const_spec(dmask.shape),
            _const_spec(xi.shape),
            _const_spec(zeta.shape),
            _const_spec(rdecay.shape),
            _const_spec(rmask.shape),
            _const_spec(retg.shape),
            _const_spec(scw.shape),
            _const_spec(cng.shape),
            _const_spec((MXU_DIM, MXU_DIM)),
            _const_spec(w_out.shape),
        ],
        out_specs=pl.BlockSpec((ts, D_MODEL), row),
        scratch_shapes=[
            pltpu.VMEM((ts, RET_WIDTH), BF16),
            pltpu.VMEM((3, RET_WIDTH, ts), BF16),
            pltpu.VMEM((3, ts, RET_WIDTH), BF16),
            pltpu.VMEM((ts, RET_WIDTH), F32),
            pltpu.VMEM((ts + SUBLANES, CONV_WIDTH), F32),
            pltpu.VMEM((HEAD_PAIRS, LANES, LANES), F32),
        ],
        compiler_params=pltpu.CompilerParams(
            dimension_semantics=("arbitrary",), vmem_limit_bytes=VMEM_LIMIT_BYTES),
        name="token_mix",
    )(x, pos_row, mod, g1, w_in, invf, dmask, xi, zeta, rdecay, rmask, retg, scw, cng,
      _group_matrix(), w_out)


def _ffn_kernel(x_ref, mod_ref, g2_ref, wup_ref, fcw_ref, fcb_ref, wdown_ref, fg_ref, o_ref, a_s):
    ts = x_ref.shape[0]

    @pl.when(pl.program_id(0) == 0)
    def _():
        a_s[0:SUBLANES, :] = jnp.zeros((SUBLANES, D_FF), F32)

    x = x_ref[...]
    h = (_rms(x, g2_ref[...]) * (1.0 + mod_ref[4:5, :]) + mod_ref[3:4, :]).astype(BF16)
    a_s[SUBLANES:ts + SUBLANES, :] = jnp.dot(h, wup_ref[:, 0:D_FF], preferred_element_type=F32)
    val = jnp.dot(h, wup_ref[:, D_FF:2 * D_FF], preferred_element_type=F32)
    a = _causal_conv3(a_s, fcw_ref, ts) + fcb_ref[...]
    act = (_silu(a) * val).astype(BF16)
    ffn = jnp.dot(act, wdown_ref[...], preferred_element_type=F32)
    x2 = x + mod_ref[5:6, :] * ffn
    o_ref[...] = _rms(x2, fg_ref[...]) * (1.0 + mod_ref[7:8, :]) + mod_ref[6:7, :]


def _ffn_call(x, mod, g2, w_up, fcw, fcb, w_down, fg):
    s = x.shape[0]
    ts = TS_FFN
    row = lambda i: (i, 0)
    return pl.pallas_call(
        _ffn_kernel,
        out_shape=jax.ShapeDtypeStruct((s, D_MODEL), F32),
        grid=(s // ts,),
        in_specs=[
            pl.BlockSpec((ts, D_MODEL), row),
            _const_spec(mod.shape),
            _const_spec(g2.shape),
            _const_spec(w_up.shape),
            _const_spec(fcw.shape),
            _const_spec(fcb.shape),
            _const_spec(w_down.shape),
            _const_spec(fg.shape),
        ],
        out_specs=pl.BlockSpec((ts, D_MODEL), row),
        scratch_shapes=[pltpu.VMEM((ts + SUBLANES, D_FF), F32)],
        compiler_params=pltpu.CompilerParams(
            dimension_semantics=("arbitrary",), vmem_limit_bytes=VMEM_LIMIT_BYTES),
        name="convglu_ffn",
    )(x, mod, g2, w_up, fcw, fcb, w_down, fg)


def kernel(x, c, positions, mod_w, mod_b, norm1_g, w_in, ret_norm_g, short_conv_w, conv_norm_g,
           w_out, norm2_g, w_up, ffn_conv_w, ffn_conv_b, w_down, final_mod_w, final_mod_b,
           final_norm_g):
    b, s, d = x.shape
    assert (b, d) == (1, D_MODEL) and mod_w.shape[0] == 1 and s % max(TS_MIX, TS_FFN) == 0
    c_col = c.reshape(d, 1)
    mod_bias = jnp.concatenate([mod_b[0], final_mod_b])[None, :]
    mod = _mod_call(c_col, mod_w[0], final_mod_w, mod_bias).reshape(8, d)

    x2d = x.reshape(s, d)
    x1 = _mix_call(x2d, positions, mod, norm1_g, w_in[0].astype(BF16), ret_norm_g,
                   short_conv_w[0], conv_norm_g, w_out[0].astype(BF16))
    out = _ffn_call(x1, mod, norm2_g, w_up[0].astype(BF16), ffn_conv_w[0], ffn_conv_b,
                    w_down[0].astype(BF16), final_norm_g[None, :])
    return out.reshape(b, s, d)
```

```python
import functools

import numpy as np
import jax
import jax.numpy as jnp
from jax import lax
from jax.experimental import pallas as pl
from jax.experimental.pallas import tpu as pltpu

F32 = jnp.float32
BF16 = jnp.bfloat16

D_MODEL = 1024
RET_WIDTH = 512
RET_HEADS = 8
HEAD_DIM = 64
CONV_WIDTH = 512
GROUP_DIM = 64
CHUNK = 128
D_FF = 2816
ROPE_BASE = 10000.0
EPS = 1e-6
IN_COLS = 4 * RET_WIDTH + 3 * CONV_WIDTH

LANES = 128
SUBLANES = 8
HEAD_PAIRS = RET_WIDTH // LANES
VMEM_LIMIT_BYTES = 56 * 1024 * 1024

TS_MIX = 512
TS_FFN = 512


def _const_spec(shape):
    return pl.BlockSpec(shape, lambda i: (0,) * len(shape), pipeline_mode=pl.Buffered(1))


def _rms(x, gain):
    ms = jnp.mean(x * x, axis=-1, keepdims=True)
    return x * lax.rsqrt(ms + EPS) * gain


def _silu(x):
    return x * jax.nn.sigmoid(x)


MOD_BLOCK = 1024


def _mod_kernel(c_ref, w1_ref, w2_ref, b_ref, o_ref, *, n1_blocks):
    ca = _silu(c_ref[...])

    @pl.when(pl.program_id(0) < n1_blocks)
    def _():
        o_ref[...] = jnp.sum(ca * w1_ref[...], axis=0, keepdims=True) + b_ref[...]

    @pl.when(pl.program_id(0) >= n1_blocks)
    def _():
        o_ref[...] = jnp.sum(ca * w2_ref[...], axis=0, keepdims=True) + b_ref[...]


def _mod_call(c_col, w1, w2, b_row):
    d, n1 = w1.shape
    n2 = w2.shape[1]
    bn = MOD_BLOCK
    nb1, nb2 = n1 // bn, n2 // bn
    return pl.pallas_call(
        functools.partial(_mod_kernel, n1_blocks=nb1),
        out_shape=jax.ShapeDtypeStruct((1, n1 + n2), F32),
        grid=(nb1 + nb2,),
        in_specs=[
            pl.BlockSpec((d, 1), lambda j: (0, 0)),
            pl.BlockSpec((d, bn), lambda j: (0, jnp.minimum(j, nb1 - 1))),
            pl.BlockSpec((d, bn), lambda j: (0, jnp.maximum(j - nb1, 0))),
            pl.BlockSpec((1, bn), lambda j: (0, j)),
        ],
        out_specs=pl.BlockSpec((1, bn), lambda j: (0, j)),
        compiler_params=pltpu.CompilerParams(
            dimension_semantics=("arbitrary",), vmem_limit_bytes=VMEM_LIMIT_BYTES),
        name="adaln_mod",
    )(c_col, w1, w2, b_row)


def _group_rms(y, gain):
    lane = lax.broadcasted_iota(jnp.int32, (y.shape[0], LANES), 1)
    group0 = (lane & GROUP_DIM) == 0
    outs = []
    for lo in range(0, y.shape[1], LANES):
        yy = y[:, lo:lo + LANES]
        y2 = yy * yy
        s0 = jnp.sum(jnp.where(group0, y2, 0.0), axis=-1, keepdims=True)
        s1 = jnp.sum(jnp.where(group0, 0.0, y2), axis=-1, keepdims=True)
        ss = jnp.where(group0, s0, s1)
        outs.append(yy * lax.rsqrt(ss * (1.0 / GROUP_DIM) + EPS))
    return jnp.concatenate(outs, axis=1) * gain


def _causal_conv3(tail_ref, w_ref, ts):
    w = w_ref[...]
    y = (tail_ref[SUBLANES - 2:ts + SUBLANES - 2, :] * w[0:1, :]
         + tail_ref[SUBLANES - 1:ts + SUBLANES - 1, :] * w[1:2, :]
         + tail_ref[SUBLANES:ts + SUBLANES, :] * w[2:3, :])
    tail_ref[0:SUBLANES, :] = tail_ref[ts:ts + SUBLANES, :]
    return y


def _mix_kernel(x_ref, pos_ref, mod_ref, g1_ref, win_ref, invf_ref, dmask_ref, xi_ref, zeta_ref,
                rdecay_ref, rmask_ref, retg_ref, scw_ref, cng_ref, wout_ref,
                o_ref, q_s, kt_s, v_s, y_s, u_s, r_s):
    ts = x_ref.shape[0]

    @pl.when(pl.program_id(0) == 0)
    def _():
        r_s[...] = jnp.zeros_like(r_s)
        u_s[0:SUBLANES, :] = jnp.zeros((SUBLANES, CONV_WIDTH), F32)

    x = x_ref[...]
    h = (_rms(x, g1_ref[...]) * (1.0 + mod_ref[1:2, :]) + mod_ref[0:1, :]).astype(BF16)

    def proj(col):
        return jnp.dot(h, win_ref[:, col:col + RET_WIDTH], preferred_element_type=F32)

    half = HEAD_DIM // 2
    ang_t = invf_ref[...] * pos_ref[...].astype(F32)
    cos_t = jnp.cos(ang_t)
    sin_t = jnp.sin(ang_t)
    trig = jnp.concatenate([cos_t, sin_t, cos_t, sin_t], axis=0).T
    lane = lax.broadcasted_iota(jnp.int32, (ts, LANES), 1)
    first_half = (lane & half) == 0
    cos = jnp.where(first_half, trig, pltpu.roll(trig, half, 1))
    sin_signed = jnp.where(first_half, -pltpu.roll(trig, LANES - half, 1), trig)

    def rotary(t):
        swapped = jnp.where(first_half, pltpu.roll(t, LANES - half, 1), pltpu.roll(t, half, 1))
        return t * cos + swapped * sin_signed

    q = proj(0)
    k = proj(RET_WIDTH)
    k_rot = []
    for p in range(HEAD_PAIRS):
        ln = slice(p * LANES, (p + 1) * LANES)
        q_s[:, ln] = rotary(q[:, ln]).astype(BF16)
        k_rot.append(rotary(k[:, ln]) * (HEAD_DIM ** -0.5))
    k_t = jnp.concatenate(k_rot, axis=1).T
    dim_head0 = (lax.broadcasted_iota(jnp.int32, k_t.shape, 0) & HEAD_DIM) == 0
    kt_s[0] = k_t.astype(BF16)
    kt_s[1] = jnp.where(dim_head0, k_t, 0.0).astype(BF16)
    kt_s[2] = jnp.where(dim_head0, 0.0, k_t).astype(BF16)
    v = proj(2 * RET_WIDTH)
    lane_head0 = (lax.broadcasted_iota(jnp.int32, v.shape, 1) & HEAD_DIM) == 0
    v_s[0] = v.astype(BF16)
    v_s[1] = jnp.where(lane_head0, v, 0.0).astype(BF16)
    v_s[2] = jnp.where(lane_head0, 0.0, v).astype(BF16)

    for c in range(ts // CHUNK):
        rows = slice(c * CHUNK, (c + 1) * CHUNK)
        for p in range(HEAD_PAIRS):
            ln = slice(p * LANES, (p + 1) * LANES)
            qc = q_s[rows, ln]
            kcat = jnp.concatenate([kt_s[1, ln, rows], kt_s[2, ln, rows]], axis=1)
            scores = jnp.dot(qc, kcat, preferred_element_type=F32) * dmask_ref[p]
            vcat = jnp.concatenate([v_s[1, rows, ln], v_s[2, rows, ln]], axis=0)
            y_inner = jnp.dot(scores.astype(BF16), vcat, preferred_element_type=F32)
            r = r_s[p]
            y_cross = jnp.dot(qc, r.astype(BF16), preferred_element_type=F32) * xi_ref[p]
            y_s[rows, ln] = y_inner + y_cross
            vz = (v_s[0, rows, ln].astype(F32) * zeta_ref[p]).astype(BF16)
            kv = jnp.dot(kt_s[0, ln, rows], vz, preferred_element_type=F32)
            r_s[p] = r * rdecay_ref[p] + kv * rmask_ref[p]

    y_ret = _group_rms(y_s[...], retg_ref[...]) * _silu(proj(3 * RET_WIDTH))

    u_s[SUBLANES:ts + SUBLANES, :] = proj(4 * RET_WIDTH + CONV_WIDTH) * proj(4 * RET_WIDTH + 2 * CONV_WIDTH)
    y_conv = proj(4 * RET_WIDTH) * _causal_conv3(u_s, scw_ref, ts)
    y_conv = _group_rms(y_conv, cng_ref[...])

    cat = jnp.concatenate([y_ret.astype(BF16), y_conv.astype(BF16)], axis=1)
    mixed = jnp.dot(cat, wout_ref[...], preferred_element_type=F32)
    o_ref[...] = x + mod_ref[2:3, :] * mixed


def _retention_tables():
    h = np.arange(RET_HEADS, dtype=np.float64)
    log_gamma = np.log1p(-np.exp2(-5.0 - h))
    idx = np.arange(CHUNK, dtype=np.float64)
    rel = idx[:, None] - idx[None, :]
    dmask = np.where(rel[None] >= 0.0,
                     np.exp(np.maximum(rel, 0.0)[None] * log_gamma[:, None, None]), 0.0)
    dmask = dmask.reshape(HEAD_PAIRS, 2, CHUNK, CHUNK).transpose(0, 2, 1, 3)
    dmask = dmask.reshape(HEAD_PAIRS, CHUNK, 2 * CHUNK)
    zeta = np.exp((CHUNK - 1.0 - idx)[None, :] * log_gamma[:, None])
    xi = np.exp((idx + 1.0)[None, :] * log_gamma[:, None])
    chunk_decay = np.exp(CHUNK * log_gamma)

    def per_lane(t):
        t = t.reshape(HEAD_PAIRS, 2, CHUNK)
        return np.repeat(np.transpose(t, (0, 2, 1)), HEAD_DIM, axis=2)

    lane_head = np.arange(LANES) // HEAD_DIM
    rmask = np.broadcast_to(lane_head[:, None] == lane_head[None, :], (HEAD_PAIRS, LANES, LANES))
    rdecay = np.repeat(chunk_decay.reshape(HEAD_PAIRS, 1, 2), HEAD_DIM, axis=2)
    rdecay = np.broadcast_to(rdecay, (HEAD_PAIRS, LANES, LANES))
    return tuple(jnp.asarray(np.ascontiguousarray(t), dtype=F32)
                 for t in (dmask, per_lane(xi), per_lane(zeta), rdecay, rmask))


def _mix_call(x, pos_row, mod, g1, w_in, retg, scw, cng, w_out):
    s = x.shape[0]
    ts = TS_MIX
    half = HEAD_DIM // 2
    invf = (ROPE_BASE ** (-jnp.arange(half, dtype=F32) / half))[:, None]
    dmask, xi, zeta, rdecay, rmask = _retention_tables()
    row = lambda i: (i, 0)
    return pl.pallas_call(
        _mix_kernel,
        out_shape=jax.ShapeDtypeStruct((s, D_MODEL), F32),
        grid=(s // ts,),
        in_specs=[
            pl.BlockSpec((ts, D_MODEL), row),
            pl.BlockSpec((1, ts), lambda i: (0, i)),
            _const_spec(mod.shape),
            _const_spec(g1.shape),
            _const_spec(w_in.shape),
            _const_spec(invf.shape),
            _const_spec(dmask.shape),
            _const_spec(xi.shape),
            _const_spec(zeta.shape),
            _const_spec(rdecay.shape),
            _const_spec(rmask.shape),
            _const_spec(retg.shape),
            _const_spec(scw.shape),
            _const_spec(cng.shape),
            _const_spec(w_out.shape),
        ],
        out_specs=pl.BlockSpec((ts, D_MODEL), row),
        scratch_shapes=[
            pltpu.VMEM((ts, RET_WIDTH), BF16),
            pltpu.VMEM((3, RET_WIDTH, ts), BF16),
            pltpu.VMEM((3, ts, RET_WIDTH), BF16),
            pltpu.VMEM((ts, RET_WIDTH), F32),
            pltpu.VMEM((ts + SUBLANES, CONV_WIDTH), F32),
            pltpu.VMEM((HEAD_PAIRS, LANES, LANES), F32),
        ],
        compiler_params=pltpu.CompilerParams(
            dimension_semantics=("arbitrary",), vmem_limit_bytes=VMEM_LIMIT_BYTES),
        name="token_mix",
    )(x, pos_row, mod, g1, w_in, invf, dmask, xi, zeta, rdecay, rmask, retg, scw, cng, w_out)


def _ffn_kernel(x_ref, mod_ref, g2_ref, wup_ref, fcw_ref, fcb_ref, wdown_ref, fg_ref, o_ref, a_s):
    ts = x_ref.shape[0]

    @pl.when(pl.program_id(0) == 0)
    def _():
        a_s[0:SUBLANES, :] = jnp.zeros((SUBLANES, D_FF), F32)

    x = x_ref[...]
    h = (_rms(x, g2_ref[...]) * (1.0 + mod_ref[4:5, :]) + mod_ref[3:4, :]).astype(BF16)
    a_s[SUBLANES:ts + SUBLANES, :] = jnp.dot(h, wup_ref[:, 0:D_FF], preferred_element_type=F32)
    val = jnp.dot(h, wup_ref[:, D_FF:2 * D_FF], preferred_element_type=F32)
    a = _causal_conv3(a_s, fcw_ref, ts) + fcb_ref[...]
    act = (_silu(a) * val).astype(BF16)
    ffn = jnp.dot(act, wdown_ref[...], preferred_element_type=F32)
    x2 = x + mod_ref[5:6, :] * ffn
    o_ref[...] = _rms(x2, fg_ref[...]) * (1.0 + mod_ref[7:8, :]) + mod_ref[6:7, :]


def _ffn_call(x, mod, g2, w_up, fcw, fcb, w_down, fg):
    s = x.shape[0]
    ts = TS_FFN
    row = lambda i: (i, 0)
    return pl.pallas_call(
        _ffn_kernel,
        out_shape=jax.ShapeDtypeStruct((s, D_MODEL), F32),
        grid=(s // ts,),
        in_specs=[
            pl.BlockSpec((ts, D_MODEL), row),
            _const_spec(mod.shape),
            _const_spec(g2.shape),
            _const_spec(w_up.shape),
            _const_spec(fcw.shape),
            _const_spec(fcb.shape),
            _const_spec(w_down.shape),
            _const_spec(fg.shape),
        ],
        out_specs=pl.BlockSpec((ts, D_MODEL), row),
        scratch_shapes=[pltpu.VMEM((ts + SUBLANES, D_FF), F32)],
        compiler_params=pltpu.CompilerParams(
            dimension_semantics=("arbitrary",), vmem_limit_bytes=VMEM_LIMIT_BYTES),
        name="convglu_ffn",
    )(x, mod, g2, w_up, fcw, fcb, w_down, fg)


def kernel(x, c, positions, mod_w, mod_b, norm1_g, w_in, ret_norm_g, short_conv_w, conv_norm_g,
           w_out, norm2_g, w_up, ffn_conv_w, ffn_conv_b, w_down, final_mod_w, final_mod_b,
           final_norm_g):
    b, s, d = x.shape
    assert (b, d) == (1, D_MODEL) and mod_w.shape[0] == 1 and s % max(TS_MIX, TS_FFN) == 0
    c_col = c.reshape(d, 1)
    mod_bias = jnp.concatenate([mod_b[0], final_mod_b])[None, :]
    mod = _mod_call(c_col, mod_w[0], final_mod_w, mod_bias).reshape(8, d)

    x2d = x.reshape(s, d)
    x1 = _mix_call(x2d, positions, mod, norm1_g, w_in[0].astype(BF16), ret_norm_g,
                   short_conv_w[0], conv_norm_g, w_out[0].astype(BF16))
    out = _ffn_call(x1, mod, norm2_g, w_up[0].astype(BF16), ffn_conv_w[0], ffn_conv_b,
                    w_down[0].astype(BF16), final_norm_g[None, :])
    return out.reshape(b, s, d)
```

```python
import functools

import numpy as np
import jax
import jax.numpy as jnp
from jax import lax
from jax.experimental import pallas as pl
from jax.experimental.pallas import tpu as pltpu

F32 = jnp.float32
BF16 = jnp.bfloat16

D_MODEL = 1024
RET_WIDTH = 512
RET_HEADS = 8
HEAD_DIM = 64
CONV_WIDTH = 512
GROUP_DIM = 64
CHUNK = 128
D_FF = 2816
ROPE_BASE = 10000.0
EPS = 1e-6
IN_COLS = 4 * RET_WIDTH + 3 * CONV_WIDTH

LANES = 128
SUBLANES = 8
HEAD_PAIRS = RET_WIDTH // LANES
VMEM_LIMIT_BYTES = 56 * 1024 * 1024

TS_MIX = 512
TS_FFN = 512
FFN_COL_CHUNK = 256
FFN_DOWN_SPLIT = 2


def _const_spec(shape):
    return pl.BlockSpec(shape, lambda i: (0,) * len(shape), pipeline_mode=pl.Buffered(1))


def _rms(x, gain):
    ms = jnp.mean(x * x, axis=-1, keepdims=True)
    return x * lax.rsqrt(ms + EPS) * gain


def _silu(x):
    return x * jax.nn.sigmoid(x)


MOD_BLOCK = 1024


def _mod_kernel(c_ref, w1_ref, w2_ref, b_ref, o_ref, *, n1_blocks):
    ca = _silu(c_ref[...])

    @pl.when(pl.program_id(0) < n1_blocks)
    def _():
        o_ref[...] = jnp.sum(ca * w1_ref[...], axis=0, keepdims=True) + b_ref[...]

    @pl.when(pl.program_id(0) >= n1_blocks)
    def _():
        o_ref[...] = jnp.sum(ca * w2_ref[...], axis=0, keepdims=True) + b_ref[...]


def _mod_call(c_col, w1, w2, b_row):
    d, n1 = w1.shape
    n2 = w2.shape[1]
    bn = MOD_BLOCK
    nb1, nb2 = n1 // bn, n2 // bn
    return pl.pallas_call(
        functools.partial(_mod_kernel, n1_blocks=nb1),
        out_shape=jax.ShapeDtypeStruct((1, n1 + n2), F32),
        grid=(nb1 + nb2,),
        in_specs=[
            pl.BlockSpec((d, 1), lambda j: (0, 0)),
            pl.BlockSpec((d, bn), lambda j: (0, jnp.minimum(j, nb1 - 1))),
            pl.BlockSpec((d, bn), lambda j: (0, jnp.maximum(j - nb1, 0))),
            pl.BlockSpec((1, bn), lambda j: (0, j)),
        ],
        out_specs=pl.BlockSpec((1, bn), lambda j: (0, j)),
        compiler_params=pltpu.CompilerParams(
            dimension_semantics=("arbitrary",), vmem_limit_bytes=VMEM_LIMIT_BYTES),
        name="adaln_mod",
    )(c_col, w1, w2, b_row)


def _group_rms(y, gain):
    lane = lax.broadcasted_iota(jnp.int32, (y.shape[0], LANES), 1)
    group0 = (lane & GROUP_DIM) == 0
    outs = []
    for lo in range(0, y.shape[1], LANES):
        yy = y[:, lo:lo + LANES]
        y2 = yy * yy
        s0 = jnp.sum(jnp.where(group0, y2, 0.0), axis=-1, keepdims=True)
        s1 = jnp.sum(jnp.where(group0, 0.0, y2), axis=-1, keepdims=True)
        ss = jnp.where(group0, s0, s1)
        outs.append(yy * lax.rsqrt(ss * (1.0 / GROUP_DIM) + EPS))
    return jnp.concatenate(outs, axis=1) * gain


def _causal_conv3(tail_ref, w_ref, ts):
    w = w_ref[...]
    y = (tail_ref[SUBLANES - 2:ts + SUBLANES - 2, :] * w[0:1, :]
         + tail_ref[SUBLANES - 1:ts + SUBLANES - 1, :] * w[1:2, :]
         + tail_ref[SUBLANES:ts + SUBLANES, :] * w[2:3, :])
    tail_ref[0:SUBLANES, :] = tail_ref[ts:ts + SUBLANES, :]
    return y


def _mix_kernel(x_ref, pos_ref, mod_ref, g1_ref, win_ref, invf_ref, dmask_ref, xi_ref, zeta_ref,
                rdecay_ref, rmask_ref, retg_ref, scw_ref, cng_ref, wout_ref,
                o_ref, q_s, kt_s, v_s, y_s, u_s, r_s):
    ts = x_ref.shape[0]

    @pl.when(pl.program_id(0) == 0)
    def _():
        r_s[...] = jnp.zeros_like(r_s)
        u_s[0:SUBLANES, :] = jnp.zeros((SUBLANES, CONV_WIDTH), F32)

    x = x_ref[...]
    h = (_rms(x, g1_ref[...]) * (1.0 + mod_ref[1:2, :]) + mod_ref[0:1, :]).astype(BF16)

    def proj(col):
        return jnp.dot(h, win_ref[:, col:col + RET_WIDTH], preferred_element_type=F32)

    half = HEAD_DIM // 2
    ang_t = invf_ref[...] * pos_ref[...].astype(F32)
    cos_t = jnp.cos(ang_t)
    sin_t = jnp.sin(ang_t)
    trig = jnp.concatenate([cos_t, sin_t, cos_t, sin_t], axis=0).T
    lane = lax.broadcasted_iota(jnp.int32, (ts, LANES), 1)
    first_half = (lane & half) == 0
    cos = jnp.where(first_half, trig, pltpu.roll(trig, half, 1))
    sin_signed = jnp.where(first_half, -pltpu.roll(trig, LANES - half, 1), trig)

    def rotary(t):
        swapped = jnp.where(first_half, pltpu.roll(t, LANES - half, 1), pltpu.roll(t, half, 1))
        return t * cos + swapped * sin_signed

    q = proj(0)
    k = proj(RET_WIDTH)
    k_rot = []
    for p in range(HEAD_PAIRS):
        ln = slice(p * LANES, (p + 1) * LANES)
        q_s[:, ln] = rotary(q[:, ln]).astype(BF16)
        k_rot.append(rotary(k[:, ln]) * (HEAD_DIM ** -0.5))
    k_t = jnp.concatenate(k_rot, axis=1).T
    dim_head0 = (lax.broadcasted_iota(jnp.int32, k_t.shape, 0) & HEAD_DIM) == 0
    kt_s[0] = k_t.astype(BF16)
    kt_s[1] = jnp.where(dim_head0, k_t, 0.0).astype(BF16)
    kt_s[2] = jnp.where(dim_head0, 0.0, k_t).astype(BF16)
    v = proj(2 * RET_WIDTH)
    lane_head0 = (lax.broadcasted_iota(jnp.int32, v.shape, 1) & HEAD_DIM) == 0
    v_s[0] = v.astype(BF16)
    v_s[1] = jnp.where(lane_head0, v, 0.0).astype(BF16)
    v_s[2] = jnp.where(lane_head0, 0.0, v).astype(BF16)

    for c in range(ts // CHUNK):
        rows = slice(c * CHUNK, (c + 1) * CHUNK)
        for p in range(HEAD_PAIRS):
            ln = slice(p * LANES, (p + 1) * LANES)
            qc = q_s[rows, ln]
            kcat = jnp.concatenate([kt_s[1, ln, rows], kt_s[2, ln, rows]], axis=1)
            scores = jnp.dot(qc, kcat, preferred_element_type=F32) * dmask_ref[p]
            vcat = jnp.concatenate([v_s[1, rows, ln], v_s[2, rows, ln]], axis=0)
            y_inner = jnp.dot(scores.astype(BF16), vcat, preferred_element_type=F32)
            r = r_s[p]
            y_cross = jnp.dot(qc, r.astype(BF16), preferred_element_type=F32) * xi_ref[p]
            y_s[rows, ln] = y_inner + y_cross
            vz = (v_s[0, rows, ln].astype(F32) * zeta_ref[p]).astype(BF16)
            kv = jnp.dot(kt_s[0, ln, rows], vz, preferred_element_type=F32)
            r_s[p] = r * rdecay_ref[p] + kv * rmask_ref[p]

    y_ret = _group_rms(y_s[...], retg_ref[...]) * _silu(proj(3 * RET_WIDTH))

    u_s[SUBLANES:ts + SUBLANES, :] = proj(4 * RET_WIDTH + CONV_WIDTH) * proj(4 * RET_WIDTH + 2 * CONV_WIDTH)
    y_conv = proj(4 * RET_WIDTH) * _causal_conv3(u_s, scw_ref, ts)
    y_conv = _group_rms(y_conv, cng_ref[...])

    cat = jnp.concatenate([y_ret.astype(BF16), y_conv.astype(BF16)], axis=1)
    mixed = jnp.dot(cat, wout_ref[...], preferred_element_type=F32)
    o_ref[...] = x + mod_ref[2:3, :] * mixed


def _retention_tables():
    h = np.arange(RET_HEADS, dtype=np.float64)
    log_gamma = np.log1p(-np.exp2(-5.0 - h))
    idx = np.arange(CHUNK, dtype=np.float64)
    rel = idx[:, None] - idx[None, :]
    dmask = np.where(rel[None] >= 0.0,
                     np.exp(np.maximum(rel, 0.0)[None] * log_gamma[:, None, None]), 0.0)
    dmask = dmask.reshape(HEAD_PAIRS, 2, CHUNK, CHUNK).transpose(0, 2, 1, 3)
    dmask = dmask.reshape(HEAD_PAIRS, CHUNK, 2 * CHUNK)
    zeta = np.exp((CHUNK - 1.0 - idx)[None, :] * log_gamma[:, None])
    xi = np.exp((idx + 1.0)[None, :] * log_gamma[:, None])
    chunk_decay = np.exp(CHUNK * log_gamma)

    def per_lane(t):
        t = t.reshape(HEAD_PAIRS, 2, CHUNK)
        return np.repeat(np.transpose(t, (0, 2, 1)), HEAD_DIM, axis=2)

    lane_head = np.arange(LANES) // HEAD_DIM
    rmask = np.broadcast_to(lane_head[:, None] == lane_head[None, :], (HEAD_PAIRS, LANES, LANES))
    rdecay = np.repeat(chunk_decay.reshape(HEAD_PAIRS, 1, 2), HEAD_DIM, axis=2)
    rdecay = np.broadcast_to(rdecay, (HEAD_PAIRS, LANES, LANES))
    return tuple(jnp.asarray(np.ascontiguousarray(t), dtype=F32)
                 for t in (dmask, per_lane(xi), per_lane(zeta), rdecay, rmask))


def _mix_call(x, pos_row, mod, g1, w_in, retg, scw, cng, w_out):
    s = x.shape[0]
    ts = TS_MIX
    half = HEAD_DIM // 2
    invf = (ROPE_BASE ** (-jnp.arange(half, dtype=F32) / half))[:, None]
    dmask, xi, zeta, rdecay, rmask = _retention_tables()
    row = lambda i: (i, 0)
    return pl.pallas_call(
        _mix_kernel,
        out_shape=jax.ShapeDtypeStruct((s, D_MODEL), F32),
        grid=(s // ts,),
        in_specs=[
            pl.BlockSpec((ts, D_MODEL), row),
            pl.BlockSpec((1, ts), lambda i: (0, i)),
            _const_spec(mod.shape),
            _const_spec(g1.shape),
            _const_spec(w_in.shape),
            _const_spec(invf.shape),
            _const_spec(dmask.shape),
            _const_spec(xi.shape),
            _const_spec(zeta.shape),
            _const_spec(rdecay.shape),
            _const_spec(rmask.shape),
            _const_spec(retg.shape),
            _const_spec(scw.shape),
            _const_spec(cng.shape),
            _const_spec(w_out.shape),
        ],
        out_specs=pl.BlockSpec((ts, D_MODEL), row),
        scratch_shapes=[
            pltpu.VMEM((ts, RET_WIDTH), BF16),
            pltpu.VMEM((3, RET_WIDTH, ts), BF16),
            pltpu.VMEM((3, ts, RET_WIDTH), BF16),
            pltpu.VMEM((ts, RET_WIDTH), F32),
            pltpu.VMEM((ts + SUBLANES, CONV_WIDTH), F32),
            pltpu.VMEM((HEAD_PAIRS, LANES, LANES), F32),
        ],
        compiler_params=pltpu.CompilerParams(
            dimension_semantics=("arbitrary",), vmem_limit_bytes=VMEM_LIMIT_BYTES),
        name="token_mix",
    )(x, pos_row, mod, g1, w_in, invf, dmask, xi, zeta, rdecay, rmask, retg, scw, cng, w_out)


def _ffn_kernel(x_ref, mod_ref, g2_ref, wup_ref, fcw_ref, fcb_ref, wdown_ref, fg_ref, o_ref,
                a_s, act_s):
    ts = x_ref.shape[0]

    @pl.when(pl.program_id(0) == 0)
    def _():
        a_s[0:SUBLANES, :] = jnp.zeros((SUBLANES, D_FF), F32)

    x = x_ref[...]
    h = (_rms(x, g2_ref[...]) * (1.0 + mod_ref[4:5, :]) + mod_ref[3:4, :]).astype(BF16)
    for lo in range(0, D_FF, FFN_COL_CHUNK):
        cols = slice(lo, lo + FFN_COL_CHUNK)
        a_s[SUBLANES:ts + SUBLANES, cols] = jnp.dot(
            h, wup_ref[:, lo:lo + FFN_COL_CHUNK], preferred_element_type=F32)
        val = jnp.dot(h, wup_ref[:, D_FF + lo:D_FF + lo + FFN_COL_CHUNK],
                      preferred_element_type=F32)
        w = fcw_ref[:, cols]
        a = (a_s[SUBLANES - 2:ts + SUBLANES - 2, cols] * w[0:1, :]
             + a_s[SUBLANES - 1:ts + SUBLANES - 1, cols] * w[1:2, :]
             + a_s[SUBLANES:ts + SUBLANES, cols] * w[2:3, :]) + fcb_ref[:, cols]
        act_s[:, cols] = (_silu(a) * val).astype(BF16)
    a_s[0:SUBLANES, :] = a_s[ts:ts + SUBLANES, :]
    sub = ts // FFN_DOWN_SPLIT
    for r in range(FFN_DOWN_SPLIT):
        rows = slice(r * sub, (r + 1) * sub)
        ffn = jnp.dot(act_s[rows, :], wdown_ref[...], preferred_element_type=F32)
        x2 = x[rows] + mod_ref[5:6, :] * ffn
        o_ref[rows, :] = _rms(x2, fg_ref[...]) * (1.0 + mod_ref[7:8, :]) + mod_ref[6:7, :]


def _ffn_call(x, mod, g2, w_up, fcw, fcb, w_down, fg):
    s = x.shape[0]
    ts = TS_FFN
    row = lambda i: (i, 0)
    return pl.pallas_call(
        _ffn_kernel,
        out_shape=jax.ShapeDtypeStruct((s, D_MODEL), F32),
        grid=(s // ts,),
        in_specs=[
            pl.BlockSpec((ts, D_MODEL), row),
            _const_spec(mod.shape),
            _const_spec(g2.shape),
            _const_spec(w_up.shape),
            _const_spec(fcw.shape),
            _const_spec(fcb.shape),
            _const_spec(w_down.shape),
            _const_spec(fg.shape),
        ],
        out_specs=pl.BlockSpec((ts, D_MODEL), row),
        scratch_shapes=[
            pltpu.VMEM((ts + SUBLANES, D_FF), F32),
            pltpu.VMEM((ts, D_FF), BF16),
        ],
        compiler_params=pltpu.CompilerParams(
            dimension_semantics=("arbitrary",), vmem_limit_bytes=VMEM_LIMIT_BYTES),
        name="convglu_ffn",
    )(x, mod, g2, w_up, fcw, fcb, w_down, fg)


def kernel(x, c, positions, mod_w, mod_b, norm1_g, w_in, ret_norm_g, short_conv_w, conv_norm_g,
           w_out, norm2_g, w_up, ffn_conv_w, ffn_conv_b, w_down, final_mod_w, final_mod_b,
           final_norm_g):
    b, s, d = x.shape
    assert (b, d) == (1, D_MODEL) and mod_w.shape[0] == 1 and s % max(TS_MIX, TS_FFN) == 0
    c_col = c.reshape(d, 1)
    mod_bias = jnp.concatenate([mod_b[0], final_mod_b])[None, :]
    mod = _mod_call(c_col, mod_w[0], final_mod_w, mod_bias).reshape(8, d)

    x2d = x.reshape(s, d)
    x1 = _mix_call(x2d, positions, mod, norm1_g, w_in[0].astype(BF16), ret_norm_g,
                   short_conv_w[0], conv_norm_g, w_out[0].astype(BF16))
    out = _ffn_call(x1, mod, norm2_g, w_up[0].astype(BF16), ffn_conv_w[0], ffn_conv_b,
                    w_down[0].astype(BF16), final_norm_g[None, :])
    return out.reshape(b, s, d)
```

```python
import functools

import numpy as np
import jax
import jax.numpy as jnp
from jax import lax
from jax.experimental import pallas as pl
from jax.experimental.pallas import tpu as pltpu

F32 = jnp.float32
BF16 = jnp.bfloat16

D_MODEL = 1024
RET_WIDTH = 512
RET_HEADS = 8
HEAD_DIM = 64
CONV_WIDTH = 512
GROUP_DIM = 64
CHUNK = 128
D_FF = 2816
ROPE_BASE = 10000.0
EPS = 1e-6
IN_COLS = 4 * RET_WIDTH + 3 * CONV_WIDTH

LANES = 128
SUBLANES = 8
HEAD_PAIRS = RET_WIDTH // LANES
VMEM_LIMIT_BYTES = 56 * 1024 * 1024

TS_MIX = 512
TS_FFN = 512
FFN_COL_CHUNK = 256
FFN_DOWN_SPLIT = 2


def _const_spec(shape):
    return pl.BlockSpec(shape, lambda i: (0,) * len(shape), pipeline_mode=pl.Buffered(1))


def _rms(x, gain):
    ms = jnp.mean(x * x, axis=-1, keepdims=True)
    return x * lax.rsqrt(ms + EPS) * gain


def _silu(x):
    return x * jax.nn.sigmoid(x)


def _wdot(act, w):
    return lax.dot_general(act, w, (((1,), (0,)), ((), ())), preferred_element_type=F32)


MOD_BLOCK = 1024


def _mod_kernel(c_ref, w1_ref, w2_ref, b_ref, o_ref, *, n1_blocks):
    ca = _silu(c_ref[...])

    @pl.when(pl.program_id(0) < n1_blocks)
    def _():
        o_ref[...] = jnp.sum(ca * w1_ref[...], axis=0, keepdims=True) + b_ref[...]

    @pl.when(pl.program_id(0) >= n1_blocks)
    def _():
        o_ref[...] = jnp.sum(ca * w2_ref[...], axis=0, keepdims=True) + b_ref[...]


def _mod_call(c_col, w1, w2, b_row):
    d, n1 = w1.shape
    n2 = w2.shape[1]
    bn = MOD_BLOCK
    nb1, nb2 = n1 // bn, n2 // bn
    return pl.pallas_call(
        functools.partial(_mod_kernel, n1_blocks=nb1),
        out_shape=jax.ShapeDtypeStruct((1, n1 + n2), F32),
        grid=(nb1 + nb2,),
        in_specs=[
            pl.BlockSpec((d, 1), lambda j: (0, 0)),
            pl.BlockSpec((d, bn), lambda j: (0, jnp.minimum(j, nb1 - 1))),
            pl.BlockSpec((d, bn), lambda j: (0, jnp.maximum(j - nb1, 0))),
            pl.BlockSpec((1, bn), lambda j: (0, j)),
        ],
        out_specs=pl.BlockSpec((1, bn), lambda j: (0, j)),
        compiler_params=pltpu.CompilerParams(
            dimension_semantics=("arbitrary",), vmem_limit_bytes=VMEM_LIMIT_BYTES),
        name="adaln_mod",
    )(c_col, w1, w2, b_row)


def _group_rms(y, gain):
    lane = lax.broadcasted_iota(jnp.int32, (y.shape[0], LANES), 1)
    group0 = (lane & GROUP_DIM) == 0
    outs = []
    for lo in range(0, y.shape[1], LANES):
        yy = y[:, lo:lo + LANES]
        y2 = yy * yy
        s0 = jnp.sum(jnp.where(group0, y2, 0.0), axis=-1, keepdims=True)
        s1 = jnp.sum(jnp.where(group0, 0.0, y2), axis=-1, keepdims=True)
        ss = jnp.where(group0, s0, s1)
        outs.append(yy * lax.rsqrt(ss * (1.0 / GROUP_DIM) + EPS))
    return jnp.concatenate(outs, axis=1) * gain


def _causal_conv3(tail_ref, w_ref, ts):
    w = w_ref[...]
    y = (tail_ref[SUBLANES - 2:ts + SUBLANES - 2, :] * w[0:1, :]
         + tail_ref[SUBLANES - 1:ts + SUBLANES - 1, :] * w[1:2, :]
         + tail_ref[SUBLANES:ts + SUBLANES, :] * w[2:3, :])
    tail_ref[0:SUBLANES, :] = tail_ref[ts:ts + SUBLANES, :]
    return y


def _mix_kernel(x_ref, pos_ref, mod_ref, g1_ref, win_ref, invf_ref, dmask_ref, xi_ref, zeta_ref,
                rdecay_ref, rmask_ref, retg_ref, scw_ref, cng_ref, wout_ref,
                o_ref, q_s, kt_s, v_s, y_s, u_s, r_s):
    ts = x_ref.shape[0]

    @pl.when(pl.program_id(0) == 0)
    def _():
        r_s[...] = jnp.zeros_like(r_s)
        u_s[0:SUBLANES, :] = jnp.zeros((SUBLANES, CONV_WIDTH), F32)

    x = x_ref[...]
    h = (_rms(x, g1_ref[...]) * (1.0 + mod_ref[1:2, :]) + mod_ref[0:1, :]).astype(BF16)

    def proj(col):
        return _wdot(h, win_ref[:, col:col + RET_WIDTH])

    half = HEAD_DIM // 2
    ang_t = invf_ref[...] * pos_ref[...].astype(F32)
    cos_t = jnp.cos(ang_t)
    sin_t = jnp.sin(ang_t)
    trig = jnp.concatenate([cos_t, sin_t, cos_t, sin_t], axis=0).T
    lane = lax.broadcasted_iota(jnp.int32, (ts, LANES), 1)
    first_half = (lane & half) == 0
    cos = jnp.where(first_half, trig, pltpu.roll(trig, half, 1))
    sin_signed = jnp.where(first_half, -pltpu.roll(trig, LANES - half, 1), trig)

    def rotary(t):
        swapped = jnp.where(first_half, pltpu.roll(t, LANES - half, 1), pltpu.roll(t, half, 1))
        return t * cos + swapped * sin_signed

    q = proj(0)
    k = proj(RET_WIDTH)
    k_rot = []
    for p in range(HEAD_PAIRS):
        ln = slice(p * LANES, (p + 1) * LANES)
        q_s[:, ln] = rotary(q[:, ln]).astype(BF16)
        k_rot.append(rotary(k[:, ln]) * (HEAD_DIM ** -0.5))
    k_t = jnp.concatenate(k_rot, axis=1).T
    dim_head0 = (lax.broadcasted_iota(jnp.int32, k_t.shape, 0) & HEAD_DIM) == 0
    kt_s[0] = k_t.astype(BF16)
    kt_s[1] = jnp.where(dim_head0, k_t, 0.0).astype(BF16)
    kt_s[2] = jnp.where(dim_head0, 0.0, k_t).astype(BF16)
    v = proj(2 * RET_WIDTH)
    lane_head0 = (lax.broadcasted_iota(jnp.int32, v.shape, 1) & HEAD_DIM) == 0
    v_s[0] = v.astype(BF16)
    v_s[1] = jnp.where(lane_head0, v, 0.0).astype(BF16)
    v_s[2] = jnp.where(lane_head0, 0.0, v).astype(BF16)

    for c in range(ts // CHUNK):
        rows = slice(c * CHUNK, (c + 1) * CHUNK)
        for p in range(HEAD_PAIRS):
            ln = slice(p * LANES, (p + 1) * LANES)
            qc = q_s[rows, ln]
            kcat = jnp.concatenate([kt_s[1, ln, rows], kt_s[2, ln, rows]], axis=1)
            scores = jnp.dot(qc, kcat, preferred_element_type=F32) * dmask_ref[p]
            vcat = jnp.concatenate([v_s[1, rows, ln], v_s[2, rows, ln]], axis=0)
            y_inner = jnp.dot(scores.astype(BF16), vcat, preferred_element_type=F32)
            r = r_s[p]
            y_cross = jnp.dot(qc, r.astype(BF16), preferred_element_type=F32) * xi_ref[p]
            y_s[rows, ln] = y_inner + y_cross
            vz = (v_s[0, rows, ln].astype(F32) * zeta_ref[p]).astype(BF16)
            kv = jnp.dot(kt_s[0, ln, rows], vz, preferred_element_type=F32)
            r_s[p] = r * rdecay_ref[p] + kv * rmask_ref[p]

    y_ret = _group_rms(y_s[...], retg_ref[...]) * _silu(proj(3 * RET_WIDTH))

    u_s[SUBLANES:ts + SUBLANES, :] = proj(4 * RET_WIDTH + CONV_WIDTH) * proj(4 * RET_WIDTH + 2 * CONV_WIDTH)
    y_conv = proj(4 * RET_WIDTH) * _causal_conv3(u_s, scw_ref, ts)
    y_conv = _group_rms(y_conv, cng_ref[...])

    cat = jnp.concatenate([y_ret.astype(BF16), y_conv.astype(BF16)], axis=1)
    mixed = _wdot(cat, wout_ref[...])
    o_ref[...] = x + mod_ref[2:3, :] * mixed


def _retention_tables():
    h = np.arange(RET_HEADS, dtype=np.float64)
    log_gamma = np.log1p(-np.exp2(-5.0 - h))
    idx = np.arange(CHUNK, dtype=np.float64)
    rel = idx[:, None] - idx[None, :]
    dmask = np.where(rel[None] >= 0.0,
                     np.exp(np.maximum(rel, 0.0)[None] * log_gamma[:, None, None]), 0.0)
    dmask = dmask.reshape(HEAD_PAIRS, 2, CHUNK, CHUNK).transpose(0, 2, 1, 3)
    dmask = dmask.reshape(HEAD_PAIRS, CHUNK, 2 * CHUNK)
    zeta = np.exp((CHUNK - 1.0 - idx)[None, :] * log_gamma[:, None])
    xi = np.exp((idx + 1.0)[None, :] * log_gamma[:, None])
    chunk_decay = np.exp(CHUNK * log_gamma)

    def per_lane(t):
        t = t.reshape(HEAD_PAIRS, 2, CHUNK)
        return np.repeat(np.transpose(t, (0, 2, 1)), HEAD_DIM, axis=2)

    lane_head = np.arange(LANES) // HEAD_DIM
    rmask = np.broadcast_to(lane_head[:, None] == lane_head[None, :], (HEAD_PAIRS, LANES, LANES))
    rdecay = np.repeat(chunk_decay.reshape(HEAD_PAIRS, 1, 2), HEAD_DIM, axis=2)
    rdecay = np.broadcast_to(rdecay, (HEAD_PAIRS, LANES, LANES))
    return tuple(jnp.asarray(np.ascontiguousarray(t), dtype=F32)
                 for t in (dmask, per_lane(xi), per_lane(zeta), rdecay, rmask))


def _mix_call(x, pos_row, mod, g1, w_in, retg, scw, cng, w_out):
    s = x.shape[0]
    ts = TS_MIX
    half = HEAD_DIM // 2
    invf = (ROPE_BASE ** (-jnp.arange(half, dtype=F32) / half))[:, None]
    dmask, xi, zeta, rdecay, rmask = _retention_tables()
    row = lambda i: (i, 0)
    return pl.pallas_call(
        _mix_kernel,
        out_shape=jax.ShapeDtypeStruct((s, D_MODEL), F32),
        grid=(s // ts,),
        in_specs=[
            pl.BlockSpec((ts, D_MODEL), row),
            pl.BlockSpec((1, ts), lambda i: (0, i)),
            _const_spec(mod.shape),
            _const_spec(g1.shape),
            _const_spec(w_in.shape),
            _const_spec(invf.shape),
            _const_spec(dmask.shape),
            _const_spec(xi.shape),
            _const_spec(zeta.shape),
            _const_spec(rdecay.shape),
            _const_spec(rmask.shape),
            _const_spec(retg.shape),
            _const_spec(scw.shape),
            _const_spec(cng.shape),
            _const_spec(w_out.shape),
        ],
        out_specs=pl.BlockSpec((ts, D_MODEL), row),
        scratch_shapes=[
            pltpu.VMEM((ts, RET_WIDTH), BF16),
            pltpu.VMEM((3, RET_WIDTH, ts), BF16),
            pltpu.VMEM((3, ts, RET_WIDTH), BF16),
            pltpu.VMEM((ts, RET_WIDTH), F32),
            pltpu.VMEM((ts + SUBLANES, CONV_WIDTH), F32),
            pltpu.VMEM((HEAD_PAIRS, LANES, LANES), F32),
        ],
        compiler_params=pltpu.CompilerParams(
            dimension_semantics=("arbitrary",), vmem_limit_bytes=VMEM_LIMIT_BYTES),
        name="token_mix",
    )(x, pos_row, mod, g1, w_in, invf, dmask, xi, zeta, rdecay, rmask, retg, scw, cng, w_out)


def _ffn_kernel(x_ref, mod_ref, g2_ref, wup_ref, fcw_ref, fcb_ref, wdown_ref, fg_ref, o_ref,
                a_s, act_s):
    ts = x_ref.shape[0]

    @pl.when(pl.program_id(0) == 0)
    def _():
        a_s[0:SUBLANES, :] = jnp.zeros((SUBLANES, D_FF), F32)

    x = x_ref[...]
    h = (_rms(x, g2_ref[...]) * (1.0 + mod_ref[4:5, :]) + mod_ref[3:4, :]).astype(BF16)
    for lo in range(0, D_FF, FFN_COL_CHUNK):
        cols = slice(lo, lo + FFN_COL_CHUNK)
        a_s[SUBLANES:ts + SUBLANES, cols] = _wdot(h, wup_ref[:, lo:lo + FFN_COL_CHUNK])
        val = _wdot(h, wup_ref[:, D_FF + lo:D_FF + lo + FFN_COL_CHUNK])
        w = fcw_ref[:, cols]
        a = (a_s[SUBLANES - 2:ts + SUBLANES - 2, cols] * w[0:1, :]
             + a_s[SUBLANES - 1:ts + SUBLANES - 1, cols] * w[1:2, :]
             + a_s[SUBLANES:ts + SUBLANES, cols] * w[2:3, :]) + fcb_ref[:, cols]
        act_s[:, cols] = (_silu(a) * val).astype(BF16)
    a_s[0:SUBLANES, :] = a_s[ts:ts + SUBLANES, :]
    sub = ts // FFN_DOWN_SPLIT
    for r in range(FFN_DOWN_SPLIT):
        rows = slice(r * sub, (r + 1) * sub)
        ffn = _wdot(act_s[rows, :], wdown_ref[...])
        x2 = x[rows] + mod_ref[5:6, :] * ffn
        o_ref[rows, :] = _rms(x2, fg_ref[...]) * (1.0 + mod_ref[7:8, :]) + mod_ref[6:7, :]


def _ffn_call(x, mod, g2, w_up, fcw, fcb, w_down, fg):
    s = x.shape[0]
    ts = TS_FFN
    row = lambda i: (i, 0)
    return pl.pallas_call(
        _ffn_kernel,
        out_shape=jax.ShapeDtypeStruct((s, D_MODEL), F32),
        grid=(s // ts,),
        in_specs=[
            pl.BlockSpec((ts, D_MODEL), row),
            _const_spec(mod.shape),
            _const_spec(g2.shape),
            _const_spec(w_up.shape),
            _const_spec(fcw.shape),
            _const_spec(fcb.shape),
            _const_spec(w_down.shape),
            _const_spec(fg.shape),
        ],
        out_specs=pl.BlockSpec((ts, D_MODEL), row),
        scratch_shapes=[
            pltpu.VMEM((ts + SUBLANES, D_FF), F32),
            pltpu.VMEM((ts, D_FF), BF16),
        ],
        compiler_params=pltpu.CompilerParams(
            dimension_semantics=("arbitrary",), vmem_limit_bytes=VMEM_LIMIT_BYTES),
        name="convglu_ffn",
    )(x, mod, g2, w_up, fcw, fcb, w_down, fg)


def kernel(x, c, positions, mod_w, mod_b, norm1_g, w_in, ret_norm_g, short_conv_w, conv_norm_g,
           w_out, norm2_g, w_up, ffn_conv_w, ffn_conv_b, w_down, final_mod_w, final_mod_b,
           final_norm_g):
    b, s, d = x.shape
    assert (b, d) == (1, D_MODEL) and mod_w.shape[0] == 1 and s % max(TS_MIX, TS_FFN) == 0
    c_col = c.reshape(d, 1)
    mod_bias = jnp.concatenate([mod_b[0], final_mod_b])[None, :]
    mod = _mod_call(c_col, mod_w[0], final_mod_w, mod_bias).reshape(8, d)

    x2d = x.reshape(s, d)
    x1 = _mix_call(x2d, positions, mod, norm1_g, w_in[0], ret_norm_g,
                   short_conv_w[0], conv_norm_g, w_out[0])
    out = _ffn_call(x1, mod, norm2_g, w_up[0], ffn_conv_w[0], ffn_conv_b,
                    w_down[0], final_norm_g[None, :])
    return out.reshape(b, s, d)
```

```python
import functools

import numpy as np
import jax
import jax.numpy as jnp
from jax import lax
from jax.experimental import pallas as pl
from jax.experimental.pallas import tpu as pltpu

F32 = jnp.float32
BF16 = jnp.bfloat16

D_MODEL = 1024
RET_WIDTH = 512
RET_HEADS = 8
HEAD_DIM = 64
CONV_WIDTH = 512
GROUP_DIM = 64
CHUNK = 128
D_FF = 2816
ROPE_BASE = 10000.0
EPS = 1e-6
IN_COLS = 4 * RET_WIDTH + 3 * CONV_WIDTH

LANES = 128
SUBLANES = 8
HEAD_PAIRS = RET_WIDTH // LANES
VMEM_LIMIT_BYTES = 61 * 1024 * 1024

TS_MIX = 1024
TS_FFN = 512
FFN_COL_CHUNK = 256
FFN_DOWN_SPLIT = 2


def _const_spec(shape):
    return pl.BlockSpec(shape, lambda i: (0,) * len(shape), pipeline_mode=pl.Buffered(1))


def _rms(x, gain):
    ms = jnp.mean(x * x, axis=-1, keepdims=True)
    return x * lax.rsqrt(ms + EPS) * gain


def _silu(x):
    return x * jax.nn.sigmoid(x)


def _wdot(act, w):
    return lax.dot_general(act, w, (((1,), (0,)), ((), ())), preferred_element_type=F32)


MOD_BLOCK = 1024


def _mod_kernel(c_ref, w1_ref, w2_ref, b_ref, o_ref, *, n1_blocks):
    ca = _silu(c_ref[...])

    @pl.when(pl.program_id(0) < n1_blocks)
    def _():
        o_ref[...] = jnp.sum(ca * w1_ref[...], axis=0, keepdims=True) + b_ref[...]

    @pl.when(pl.program_id(0) >= n1_blocks)
    def _():
        o_ref[...] = jnp.sum(ca * w2_ref[...], axis=0, keepdims=True) + b_ref[...]


def _mod_call(c_col, w1, w2, b_row):
    d, n1 = w1.shape
    n2 = w2.shape[1]
    bn = MOD_BLOCK
    nb1, nb2 = n1 // bn, n2 // bn
    return pl.pallas_call(
        functools.partial(_mod_kernel, n1_blocks=nb1),
        out_shape=jax.ShapeDtypeStruct((1, n1 + n2), F32),
        grid=(nb1 + nb2,),
        in_specs=[
            pl.BlockSpec((d, 1), lambda j: (0, 0)),
            pl.BlockSpec((d, bn), lambda j: (0, jnp.minimum(j, nb1 - 1))),
            pl.BlockSpec((d, bn), lambda j: (0, jnp.maximum(j - nb1, 0))),
            pl.BlockSpec((1, bn), lambda j: (0, j)),
        ],
        out_specs=pl.BlockSpec((1, bn), lambda j: (0, j)),
        compiler_params=pltpu.CompilerParams(
            dimension_semantics=("arbitrary",), vmem_limit_bytes=VMEM_LIMIT_BYTES),
        name="adaln_mod",
    )(c_col, w1, w2, b_row)


def _group_rms(y, gain):
    lane = lax.broadcasted_iota(jnp.int32, (y.shape[0], LANES), 1)
    group0 = (lane & GROUP_DIM) == 0
    outs = []
    for lo in range(0, y.shape[1], LANES):
        yy = y[:, lo:lo + LANES]
        y2 = yy * yy
        s0 = jnp.sum(jnp.where(group0, y2, 0.0), axis=-1, keepdims=True)
        s1 = jnp.sum(jnp.where(group0, 0.0, y2), axis=-1, keepdims=True)
        ss = jnp.where(group0, s0, s1)
        outs.append(yy * lax.rsqrt(ss * (1.0 / GROUP_DIM) + EPS))
    return jnp.concatenate(outs, axis=1) * gain


def _causal_conv3(tail_ref, w_ref, ts):
    w = w_ref[...]
    y = (tail_ref[SUBLANES - 2:ts + SUBLANES - 2, :] * w[0:1, :]
         + tail_ref[SUBLANES - 1:ts + SUBLANES - 1, :] * w[1:2, :]
         + tail_ref[SUBLANES:ts + SUBLANES, :] * w[2:3, :])
    tail_ref[0:SUBLANES, :] = tail_ref[ts:ts + SUBLANES, :]
    return y


def _mix_kernel(x_ref, pos_ref, mod_ref, g1_ref, win_ref, invf_ref, dmask_ref, xi_ref, zeta_ref,
                rdecay_ref, rmask_ref, retg_ref, scw_ref, cng_ref, wout_ref, wup_ref, wdown_ref,
                o_ref, wup_bf_ref, wdown_bf_ref, q_s, kt_s, v_s, y_s, u_s, r_s):
    ts = x_ref.shape[0]

    @pl.when(pl.program_id(0) == 0)
    def _():
        r_s[...] = jnp.zeros_like(r_s)
        u_s[0:SUBLANES, :] = jnp.zeros((SUBLANES, CONV_WIDTH), F32)

    x = x_ref[...]
    h = (_rms(x, g1_ref[...]) * (1.0 + mod_ref[1:2, :]) + mod_ref[0:1, :]).astype(BF16)

    def proj(col):
        return _wdot(h, win_ref[:, col:col + RET_WIDTH])

    half = HEAD_DIM // 2
    ang_t = invf_ref[...] * pos_ref[...].astype(F32)
    cos_t = jnp.cos(ang_t)
    sin_t = jnp.sin(ang_t)
    trig = jnp.concatenate([cos_t, sin_t, cos_t, sin_t], axis=0).T
    lane = lax.broadcasted_iota(jnp.int32, (ts, LANES), 1)
    first_half = (lane & half) == 0
    cos = jnp.where(first_half, trig, pltpu.roll(trig, half, 1))
    sin_signed = jnp.where(first_half, -pltpu.roll(trig, LANES - half, 1), trig)

    def rotary(t):
        swapped = jnp.where(first_half, pltpu.roll(t, LANES - half, 1), pltpu.roll(t, half, 1))
        return t * cos + swapped * sin_signed

    q = proj(0)
    k = proj(RET_WIDTH)
    k_rot = []
    for p in range(HEAD_PAIRS):
        ln = slice(p * LANES, (p + 1) * LANES)
        q_s[:, ln] = rotary(q[:, ln]).astype(BF16)
        k_rot.append(rotary(k[:, ln]) * (HEAD_DIM ** -0.5))
    k_t = jnp.concatenate(k_rot, axis=1).T
    dim_head0 = (lax.broadcasted_iota(jnp.int32, k_t.shape, 0) & HEAD_DIM) == 0
    kt_s[0] = k_t.astype(BF16)
    kt_s[1] = jnp.where(dim_head0, k_t, 0.0).astype(BF16)
    kt_s[2] = jnp.where(dim_head0, 0.0, k_t).astype(BF16)
    v = proj(2 * RET_WIDTH)
    lane_head0 = (lax.broadcasted_iota(jnp.int32, v.shape, 1) & HEAD_DIM) == 0
    v_s[0] = v.astype(BF16)
    v_s[1] = jnp.where(lane_head0, v, 0.0).astype(BF16)
    v_s[2] = jnp.where(lane_head0, 0.0, v).astype(BF16)

    for c in range(ts // CHUNK):
        rows = slice(c * CHUNK, (c + 1) * CHUNK)
        for p in range(HEAD_PAIRS):
            ln = slice(p * LANES, (p + 1) * LANES)
            qc = q_s[rows, ln]
            kcat = jnp.concatenate([kt_s[1, ln, rows], kt_s[2, ln, rows]], axis=1)
            scores = jnp.dot(qc, kcat, preferred_element_type=F32) * dmask_ref[p]
            vcat = jnp.concatenate([v_s[1, rows, ln], v_s[2, rows, ln]], axis=0)
            y_inner = jnp.dot(scores.astype(BF16), vcat, preferred_element_type=F32)
            r = r_s[p]
            y_cross = jnp.dot(qc, r.astype(BF16), preferred_element_type=F32) * xi_ref[p]
            y_s[rows, ln] = y_inner + y_cross
            vz = (v_s[0, rows, ln].astype(F32) * zeta_ref[p]).astype(BF16)
            kv = jnp.dot(kt_s[0, ln, rows], vz, preferred_element_type=F32)
            r_s[p] = r * rdecay_ref[p] + kv * rmask_ref[p]

    y_ret = _group_rms(y_s[...], retg_ref[...]) * _silu(proj(3 * RET_WIDTH))

    u_s[SUBLANES:ts + SUBLANES, :] = proj(4 * RET_WIDTH + CONV_WIDTH) * proj(4 * RET_WIDTH + 2 * CONV_WIDTH)
    y_conv = proj(4 * RET_WIDTH) * _causal_conv3(u_s, scw_ref, ts)
    y_conv = _group_rms(y_conv, cng_ref[...])

    cat = jnp.concatenate([y_ret.astype(BF16), y_conv.astype(BF16)], axis=1)
    mixed = _wdot(cat, wout_ref[...])
    o_ref[...] = x + mod_ref[2:3, :] * mixed

    wup_bf_ref[...] = wup_ref[...].astype(BF16)
    wdown_bf_ref[...] = wdown_ref[...].astype(BF16)


def _retention_tables():
    h = np.arange(RET_HEADS, dtype=np.float64)
    log_gamma = np.log1p(-np.exp2(-5.0 - h))
    idx = np.arange(CHUNK, dtype=np.float64)
    rel = idx[:, None] - idx[None, :]
    dmask = np.where(rel[None] >= 0.0,
                     np.exp(np.maximum(rel, 0.0)[None] * log_gamma[:, None, None]), 0.0)
    dmask = dmask.reshape(HEAD_PAIRS, 2, CHUNK, CHUNK).transpose(0, 2, 1, 3)
    dmask = dmask.reshape(HEAD_PAIRS, CHUNK, 2 * CHUNK)
    zeta = np.exp((CHUNK - 1.0 - idx)[None, :] * log_gamma[:, None])
    xi = np.exp((idx + 1.0)[None, :] * log_gamma[:, None])
    chunk_decay = np.exp(CHUNK * log_gamma)

    def per_lane(t):
        t = t.reshape(HEAD_PAIRS, 2, CHUNK)
        return np.repeat(np.transpose(t, (0, 2, 1)), HEAD_DIM, axis=2)

    lane_head = np.arange(LANES) // HEAD_DIM
    rmask = np.broadcast_to(lane_head[:, None] == lane_head[None, :], (HEAD_PAIRS, LANES, LANES))
    rdecay = np.repeat(chunk_decay.reshape(HEAD_PAIRS, 1, 2), HEAD_DIM, axis=2)
    rdecay = np.broadcast_to(rdecay, (HEAD_PAIRS, LANES, LANES))
    return tuple(jnp.asarray(np.ascontiguousarray(t), dtype=F32)
                 for t in (dmask, per_lane(xi), per_lane(zeta), rdecay, rmask))


def _side_cast_spec(w, n_steps):
    k, n = w.shape
    nblk = next(b for b in range(n_steps, 0, -1)
                if k % b == 0 and (k // b) % (2 * SUBLANES) == 0)
    return pl.BlockSpec((k // nblk, n), lambda i: (jnp.minimum(i, nblk - 1), 0))


def _mix_call(x, pos_row, mod, g1, w_in, retg, scw, cng, w_out, w_up, w_down):
    s = x.shape[0]
    ts = TS_MIX
    half = HEAD_DIM // 2
    invf = (ROPE_BASE ** (-jnp.arange(half, dtype=F32) / half))[:, None]
    dmask, xi, zeta, rdecay, rmask = _retention_tables()
    row = lambda i: (i, 0)
    wup_spec = _side_cast_spec(w_up, s // ts)
    wdown_spec = _side_cast_spec(w_down, s // ts)
    return pl.pallas_call(
        _mix_kernel,
        out_shape=(jax.ShapeDtypeStruct((s, D_MODEL), F32),
                   jax.ShapeDtypeStruct(w_up.shape, BF16),
                   jax.ShapeDtypeStruct(w_down.shape, BF16)),
        grid=(s // ts,),
        in_specs=[
            pl.BlockSpec((ts, D_MODEL), row),
            pl.BlockSpec((1, ts), lambda i: (0, i)),
            _const_spec(mod.shape),
            _const_spec(g1.shape),
            _const_spec(w_in.shape),
            _const_spec(invf.shape),
            _const_spec(dmask.shape),
            _const_spec(xi.shape),
            _const_spec(zeta.shape),
            _const_spec(rdecay.shape),
            _const_spec(rmask.shape),
            _const_spec(retg.shape),
            _const_spec(scw.shape),
            _const_spec(cng.shape),
            _const_spec(w_out.shape),
            wup_spec,
            wdown_spec,
        ],
        out_specs=(pl.BlockSpec((ts, D_MODEL), row), wup_spec, wdown_spec),
        scratch_shapes=[
            pltpu.VMEM((ts, RET_WIDTH), BF16),
            pltpu.VMEM((3, RET_WIDTH, ts), BF16),
            pltpu.VMEM((3, ts, RET_WIDTH), BF16),
            pltpu.VMEM((ts, RET_WIDTH), F32),
            pltpu.VMEM((ts + SUBLANES, CONV_WIDTH), F32),
            pltpu.VMEM((HEAD_PAIRS, LANES, LANES), F32),
        ],
        compiler_params=pltpu.CompilerParams(
            dimension_semantics=("arbitrary",), vmem_limit_bytes=VMEM_LIMIT_BYTES),
        name="token_mix",
    )(x, pos_row, mod, g1, w_in, invf, dmask, xi, zeta, rdecay, rmask, retg, scw, cng, w_out,
      w_up, w_down)


def _ffn_kernel(x_ref, mod_ref, g2_ref, wup_ref, fcw_ref, fcb_ref, wdown_ref, fg_ref, o_ref,
                a_s, act_s):
    ts = x_ref.shape[0]

    @pl.when(pl.program_id(0) == 0)
    def _():
        a_s[0:SUBLANES, :] = jnp.zeros((SUBLANES, D_FF), F32)

    x = x_ref[...]
    h = (_rms(x, g2_ref[...]) * (1.0 + mod_ref[4:5, :]) + mod_ref[3:4, :]).astype(BF16)
    for lo in range(0, D_FF, FFN_COL_CHUNK):
        cols = slice(lo, lo + FFN_COL_CHUNK)
        a_s[SUBLANES:ts + SUBLANES, cols] = _wdot(h, wup_ref[:, lo:lo + FFN_COL_CHUNK])
        val = _wdot(h, wup_ref[:, D_FF + lo:D_FF + lo + FFN_COL_CHUNK])
        w = fcw_ref[:, cols]
        a = (a_s[SUBLANES - 2:ts + SUBLANES - 2, cols] * w[0:1, :]
             + a_s[SUBLANES - 1:ts + SUBLANES - 1, cols] * w[1:2, :]
             + a_s[SUBLANES:ts + SUBLANES, cols] * w[2:3, :]) + fcb_ref[:, cols]
        act_s[:, cols] = (_silu(a) * val).astype(BF16)
    a_s[0:SUBLANES, :] = a_s[ts:ts + SUBLANES, :]
    sub = ts // FFN_DOWN_SPLIT
    for r in range(FFN_DOWN_SPLIT):
        rows = slice(r * sub, (r + 1) * sub)
        ffn = _wdot(act_s[rows, :], wdown_ref[...])
        x2 = x[rows] + mod_ref[5:6, :] * ffn
        o_ref[rows, :] = _rms(x2, fg_ref[...]) * (1.0 + mod_ref[7:8, :]) + mod_ref[6:7, :]


def _ffn_call(x, mod, g2, w_up, fcw, fcb, w_down, fg):
    s = x.shape[0]
    ts = TS_FFN
    row = lambda i: (i, 0)
    return pl.pallas_call(
        _ffn_kernel,
        out_shape=jax.ShapeDtypeStruct((s, D_MODEL), F32),
        grid=(s // ts,),
        in_specs=[
            pl.BlockSpec((ts, D_MODEL), row),
            _const_spec(mod.shape),
            _const_spec(g2.shape),
            _const_spec(w_up.shape),
            _const_spec(fcw.shape),
            _const_spec(fcb.shape),
            _const_spec(w_down.shape),
            _const_spec(fg.shape),
        ],
        out_specs=pl.BlockSpec((ts, D_MODEL), row),
        scratch_shapes=[
            pltpu.VMEM((ts + SUBLANES, D_FF), F32),
            pltpu.VMEM((ts, D_FF), BF16),
        ],
        compiler_params=pltpu.CompilerParams(
            dimension_semantics=("arbitrary",), vmem_limit_bytes=VMEM_LIMIT_BYTES),
        name="convglu_ffn",
    )(x, mod, g2, w_up, fcw, fcb, w_down, fg)


def kernel(x, c, positions, mod_w, mod_b, norm1_g, w_in, ret_norm_g, short_conv_w, conv_norm_g,
           w_out, norm2_g, w_up, ffn_conv_w, ffn_conv_b, w_down, final_mod_w, final_mod_b,
           final_norm_g):
    b, s, d = x.shape
    assert (b, d) == (1, D_MODEL) and mod_w.shape[0] == 1 and s % max(TS_MIX, TS_FFN) == 0
    c_col = c.reshape(d, 1)
    mod_bias = jnp.concatenate([mod_b[0], final_mod_b])[None, :]
    mod = _mod_call(c_col, mod_w[0], final_mod_w, mod_bias).reshape(8, d)

    x2d = x.reshape(s, d)
    x1, w_up_bf, w_down_bf = _mix_call(x2d, positions, mod, norm1_g, w_in[0], ret_norm_g,
                                       short_conv_w[0], conv_norm_g, w_out[0], w_up[0], w_down[0])
    out = _ffn_call(x1, mod, norm2_g, w_up_bf, ffn_conv_w[0], ffn_conv_b, w_down_bf,
                    final_norm_g[None, :])
    return out.reshape(b, s, d)
```

```python
import functools

import numpy as np
import jax
import jax.numpy as jnp
from jax import lax
from jax.experimental import pallas as pl
from jax.experimental.pallas import tpu as pltpu

F32 = jnp.float32
BF16 = jnp.bfloat16

D_MODEL = 1024
RET_WIDTH = 512
RET_HEADS = 8
HEAD_DIM = 64
CONV_WIDTH = 512
GROUP_DIM = 64
CHUNK = 128
D_FF = 2816
ROPE_BASE = 10000.0
EPS = 1e-6
IN_COLS = 4 * RET_WIDTH + 3 * CONV_WIDTH

LANES = 128
SUBLANES = 8
HEAD_PAIRS = RET_WIDTH // LANES
VMEM_LIMIT_BYTES = 56 * 1024 * 1024

TS_MIX = 512
TS_FFN = 512
FFN_COL_CHUNK = 256
FFN_DOWN_SPLIT = 2


def _const_spec(shape):
    return pl.BlockSpec(shape, lambda i: (0,) * len(shape), pipeline_mode=pl.Buffered(1))


def _rms(x, gain):
    ms = jnp.mean(x * x, axis=-1, keepdims=True)
    return x * lax.rsqrt(ms + EPS) * gain


def _silu(x):
    return x * jax.nn.sigmoid(x)


def _wdot(act, w):
    return lax.dot_general(act, w, (((1,), (0,)), ((), ())), preferred_element_type=F32)


MOD_BLOCK = 512
SIDE_MOD_BLOCK = 256


def _mod_matvec(c_col, w, b):
    return jnp.sum(_silu(c_col) * w, axis=0, keepdims=True) + b


def _mod_kernel(c_ref, w_ref, b_ref, o_ref):
    o_ref[...] = _mod_matvec(c_ref[...], w_ref[...], b_ref[...])


def _mod_call(c_col, w, b_row, n_cols):
    d = w.shape[0]
    bn = MOD_BLOCK
    return pl.pallas_call(
        _mod_kernel,
        out_shape=jax.ShapeDtypeStruct((1, n_cols), F32),
        grid=(n_cols // bn,),
        in_specs=[
            pl.BlockSpec((d, 1), lambda j: (0, 0)),
            pl.BlockSpec((d, bn), lambda j: (0, j)),
            pl.BlockSpec((1, bn), lambda j: (0, j)),
        ],
        out_specs=pl.BlockSpec((1, bn), lambda j: (0, j)),
        compiler_params=pltpu.CompilerParams(
            dimension_semantics=("arbitrary",), vmem_limit_bytes=VMEM_LIMIT_BYTES),
        name="adaln_mod",
    )(c_col, w, b_row)


def _group_rms(y, gain):
    lane = lax.broadcasted_iota(jnp.int32, (y.shape[0], LANES), 1)
    group0 = (lane & GROUP_DIM) == 0
    outs = []
    for lo in range(0, y.shape[1], LANES):
        yy = y[:, lo:lo + LANES]
        y2 = yy * yy
        s0 = jnp.sum(jnp.where(group0, y2, 0.0), axis=-1, keepdims=True)
        s1 = jnp.sum(jnp.where(group0, 0.0, y2), axis=-1, keepdims=True)
        ss = jnp.where(group0, s0, s1)
        outs.append(yy * lax.rsqrt(ss * (1.0 / GROUP_DIM) + EPS))
    return jnp.concatenate(outs, axis=1) * gain


def _causal_conv3(tail_ref, w_ref, ts):
    w = w_ref[...]
    y = (tail_ref[SUBLANES - 2:ts + SUBLANES - 2, :] * w[0:1, :]
         + tail_ref[SUBLANES - 1:ts + SUBLANES - 1, :] * w[1:2, :]
         + tail_ref[SUBLANES:ts + SUBLANES, :] * w[2:3, :])
    tail_ref[0:SUBLANES, :] = tail_ref[ts:ts + SUBLANES, :]
    return y


def _mix_kernel(x_ref, pos_ref, mod_ref, g1_ref, win_ref, invf_ref, dmask_ref, xi_ref, zeta_ref,
                rdecay_ref, rmask_ref, retg_ref, scw_ref, cng_ref, wout_ref, wup_ref, wdown_ref,
                c_ref, modw_ref, modb_ref, fmodw_ref, fmodb_ref,
                o_ref, wup_bf_ref, wdown_bf_ref, mod2_ref, fmod_ref, q_s, kt_s, v_s, y_s, u_s, r_s):
    ts = x_ref.shape[0]

    @pl.when(pl.program_id(0) == 0)
    def _():
        r_s[...] = jnp.zeros_like(r_s)
        u_s[0:SUBLANES, :] = jnp.zeros((SUBLANES, CONV_WIDTH), F32)

    x = x_ref[...]
    shift1 = mod_ref[:, 0:D_MODEL]
    scale1 = mod_ref[:, D_MODEL:2 * D_MODEL]
    gate1 = mod_ref[:, 2 * D_MODEL:3 * D_MODEL]
    h = (_rms(x, g1_ref[...]) * (1.0 + scale1) + shift1).astype(BF16)

    def proj(col):
        return _wdot(h, win_ref[:, col:col + RET_WIDTH])

    half = HEAD_DIM // 2
    ang_t = invf_ref[...] * pos_ref[...].astype(F32)
    cos_t = jnp.cos(ang_t)
    sin_t = jnp.sin(ang_t)
    trig = jnp.concatenate([cos_t, sin_t, cos_t, sin_t], axis=0).T
    lane = lax.broadcasted_iota(jnp.int32, (ts, LANES), 1)
    first_half = (lane & half) == 0
    cos = jnp.where(first_half, trig, pltpu.roll(trig, half, 1))
    sin_signed = jnp.where(first_half, -pltpu.roll(trig, LANES - half, 1), trig)

    def rotary(t):
        swapped = jnp.where(first_half, pltpu.roll(t, LANES - half, 1), pltpu.roll(t, half, 1))
        return t * cos + swapped * sin_signed

    q = proj(0)
    k = proj(RET_WIDTH)
    k_rot = []
    for p in range(HEAD_PAIRS):
        ln = slice(p * LANES, (p + 1) * LANES)
        q_s[:, ln] = rotary(q[:, ln]).astype(BF16)
        k_rot.append(rotary(k[:, ln]) * (HEAD_DIM ** -0.5))
    k_t = jnp.concatenate(k_rot, axis=1).T
    dim_head0 = (lax.broadcasted_iota(jnp.int32, k_t.shape, 0) & HEAD_DIM) == 0
    kt_s[0] = k_t.astype(BF16)
    kt_s[1] = jnp.where(dim_head0, k_t, 0.0).astype(BF16)
    kt_s[2] = jnp.where(dim_head0, 0.0, k_t).astype(BF16)
    v = proj(2 * RET_WIDTH)
    lane_head0 = (lax.broadcasted_iota(jnp.int32, v.shape, 1) & HEAD_DIM) == 0
    v_s[0] = v.astype(BF16)
    v_s[1] = jnp.where(lane_head0, v, 0.0).astype(BF16)
    v_s[2] = jnp.where(lane_head0, 0.0, v).astype(BF16)

    for c in range(ts // CHUNK):
        rows = slice(c * CHUNK, (c + 1) * CHUNK)
        for p in range(HEAD_PAIRS):
            ln = slice(p * LANES, (p + 1) * LANES)
            qc = q_s[rows, ln]
            kcat = jnp.concatenate([kt_s[1, ln, rows], kt_s[2, ln, rows]], axis=1)
            scores = jnp.dot(qc, kcat, preferred_element_type=F32) * dmask_ref[p]
            vcat = jnp.concatenate([v_s[1, rows, ln], v_s[2, rows, ln]], axis=0)
            y_inner = jnp.dot(scores.astype(BF16), vcat, preferred_element_type=F32)
            r = r_s[p]
            y_cross = jnp.dot(qc, r.astype(BF16), preferred_element_type=F32) * xi_ref[p]
            y_s[rows, ln] = y_inner + y_cross
            vz = (v_s[0, rows, ln].astype(F32) * zeta_ref[p]).astype(BF16)
            kv = jnp.dot(kt_s[0, ln, rows], vz, preferred_element_type=F32)
            r_s[p] = r * rdecay_ref[p] + kv * rmask_ref[p]

    y_ret = _group_rms(y_s[...], retg_ref[...]) * _silu(proj(3 * RET_WIDTH))

    u_s[SUBLANES:ts + SUBLANES, :] = proj(4 * RET_WIDTH + CONV_WIDTH) * proj(4 * RET_WIDTH + 2 * CONV_WIDTH)
    y_conv = proj(4 * RET_WIDTH) * _causal_conv3(u_s, scw_ref, ts)
    y_conv = _group_rms(y_conv, cng_ref[...])

    cat = jnp.concatenate([y_ret.astype(BF16), y_conv.astype(BF16)], axis=1)
    mixed = _wdot(cat, wout_ref[...])
    o_ref[...] = x + gate1 * mixed

    wup_bf_ref[...] = wup_ref[...].astype(BF16)
    wdown_bf_ref[...] = wdown_ref[...].astype(BF16)
    mod2_ref[...] = _mod_matvec(c_ref[...], modw_ref[...], modb_ref[...])
    fmod_ref[...] = _mod_matvec(c_ref[...], fmodw_ref[...], fmodb_ref[...])


def _retention_tables():
    h = np.arange(RET_HEADS, dtype=np.float64)
    log_gamma = np.log1p(-np.exp2(-5.0 - h))
    idx = np.arange(CHUNK, dtype=np.float64)
    rel = idx[:, None] - idx[None, :]
    dmask = np.where(rel[None] >= 0.0,
                     np.exp(np.maximum(rel, 0.0)[None] * log_gamma[:, None, None]), 0.0)
    dmask = dmask.reshape(HEAD_PAIRS, 2, CHUNK, CHUNK).transpose(0, 2, 1, 3)
    dmask = dmask.reshape(HEAD_PAIRS, CHUNK, 2 * CHUNK)
    zeta = np.exp((CHUNK - 1.0 - idx)[None, :] * log_gamma[:, None])
    xi = np.exp((idx + 1.0)[None, :] * log_gamma[:, None])
    chunk_decay = np.exp(CHUNK * log_gamma)

    def per_lane(t):
        t = t.reshape(HEAD_PAIRS, 2, CHUNK)
        return np.repeat(np.transpose(t, (0, 2, 1)), HEAD_DIM, axis=2)

    lane_head = np.arange(LANES) // HEAD_DIM
    rmask = np.broadcast_to(lane_head[:, None] == lane_head[None, :], (HEAD_PAIRS, LANES, LANES))
    rdecay = np.repeat(chunk_decay.reshape(HEAD_PAIRS, 1, 2), HEAD_DIM, axis=2)
    rdecay = np.broadcast_to(rdecay, (HEAD_PAIRS, LANES, LANES))
    return tuple(jnp.asarray(np.ascontiguousarray(t), dtype=F32)
                 for t in (dmask, per_lane(xi), per_lane(zeta), rdecay, rmask))


def _side_cast_spec(w, n_steps):
    k, n = w.shape
    nblk = next(b for b in range(n_steps, 0, -1)
                if k % b == 0 and (k // b) % (2 * SUBLANES) == 0)
    return pl.BlockSpec((k // nblk, n), lambda i: (jnp.minimum(i, nblk - 1), 0))


def _side_cols_spec(rows, n_cols, first_col, n_steps):
    nblk = n_cols // SIDE_MOD_BLOCK
    assert nblk <= n_steps and n_cols % SIDE_MOD_BLOCK == 0 and first_col % SIDE_MOD_BLOCK == 0
    first = first_col // SIDE_MOD_BLOCK
    return pl.BlockSpec((rows, SIDE_MOD_BLOCK), lambda i: (0, first + jnp.minimum(i, nblk - 1)))


def _mix_call(x, pos_row, mod1, g1, w_in, retg, scw, cng, w_out, w_up, w_down,
              c_col, mod_w, mod_b, fmod_w, fmod_b):
    s = x.shape[0]
    ts = TS_MIX
    n_steps = s // ts
    half = HEAD_DIM // 2
    invf = (ROPE_BASE ** (-jnp.arange(half, dtype=F32) / half))[:, None]
    dmask, xi, zeta, rdecay, rmask = _retention_tables()
    row = lambda i: (i, 0)
    wup_spec = _side_cast_spec(w_up, n_steps)
    wdown_spec = _side_cast_spec(w_down, n_steps)
    n_mod1 = mod1.shape[1]
    n_mod2 = mod_w.shape[1] - n_mod1
    n_fmod = fmod_w.shape[1]
    return pl.pallas_call(
        _mix_kernel,
        out_shape=(jax.ShapeDtypeStruct((s, D_MODEL), F32),
                   jax.ShapeDtypeStruct(w_up.shape, BF16),
                   jax.ShapeDtypeStruct(w_down.shape, BF16),
                   jax.ShapeDtypeStruct((1, n_mod2), F32),
                   jax.ShapeDtypeStruct((1, n_fmod), F32)),
        grid=(n_steps,),
        in_specs=[
            pl.BlockSpec((ts, D_MODEL), row),
            pl.BlockSpec((1, ts), lambda i: (0, i)),
            _const_spec(mod1.shape),
            _const_spec(g1.shape),
            _const_spec(w_in.shape),
            _const_spec(invf.shape),
            _const_spec(dmask.shape),
            _const_spec(xi.shape),
            _const_spec(zeta.shape),
            _const_spec(rdecay.shape),
            _const_spec(rmask.shape),
            _const_spec(retg.shape),
            _const_spec(scw.shape),
            _const_spec(cng.shape),
            _const_spec(w_out.shape),
            wup_spec,
            wdown_spec,
            _const_spec(c_col.shape),
            _side_cols_spec(mod_w.shape[0], n_mod2, n_mod1, n_steps),
            _side_cols_spec(1, n_mod2, n_mod1, n_steps),
            _side_cols_spec(fmod_w.shape[0], n_fmod, 0, n_steps),
            _side_cols_spec(1, n_fmod, 0, n_steps),
        ],
        out_specs=(pl.BlockSpec((ts, D_MODEL), row), wup_spec, wdown_spec,
                   _side_cols_spec(1, n_mod2, 0, n_steps), _side_cols_spec(1, n_fmod, 0, n_steps)),
        scratch_shapes=[
            pltpu.VMEM((ts, RET_WIDTH), BF16),
            pltpu.VMEM((3, RET_WIDTH, ts), BF16),
            pltpu.VMEM((3, ts, RET_WIDTH), BF16),
            pltpu.VMEM((ts, RET_WIDTH), F32),
            pltpu.VMEM((ts + SUBLANES, CONV_WIDTH), F32),
            pltpu.VMEM((HEAD_PAIRS, LANES, LANES), F32),
        ],
        compiler_params=pltpu.CompilerParams(
            dimension_semantics=("arbitrary",), vmem_limit_bytes=VMEM_LIMIT_BYTES),
        name="token_mix",
    )(x, pos_row, mod1, g1, w_in, invf, dmask, xi, zeta, rdecay, rmask, retg, scw, cng, w_out,
      w_up, w_down, c_col, mod_w, mod_b, fmod_w, fmod_b)


def _ffn_kernel(x_ref, mod_ref, fmod_ref, g2_ref, wup_ref, fcw_ref, fcb_ref, wdown_ref, fg_ref,
                o_ref, a_s, act_s):
    ts = x_ref.shape[0]

    @pl.when(pl.program_id(0) == 0)
    def _():
        a_s[0:SUBLANES, :] = jnp.zeros((SUBLANES, D_FF), F32)

    shift2 = mod_ref[:, 0:D_MODEL]
    scale2 = mod_ref[:, D_MODEL:2 * D_MODEL]
    gate2 = mod_ref[:, 2 * D_MODEL:3 * D_MODEL]
    final_shift = fmod_ref[:, 0:D_MODEL]
    final_scale = fmod_ref[:, D_MODEL:2 * D_MODEL]
    x = x_ref[...]
    h = (_rms(x, g2_ref[...]) * (1.0 + scale2) + shift2).astype(BF16)
    for lo in range(0, D_FF, FFN_COL_CHUNK):
        cols = slice(lo, lo + FFN_COL_CHUNK)
        a_s[SUBLANES:ts + SUBLANES, cols] = _wdot(h, wup_ref[:, lo:lo + FFN_COL_CHUNK])
        val = _wdot(h, wup_ref[:, D_FF + lo:D_FF + lo + FFN_COL_CHUNK])
        w = fcw_ref[:, cols]
        a = (a_s[SUBLANES - 2:ts + SUBLANES - 2, cols] * w[0:1, :]
             + a_s[SUBLANES - 1:ts + SUBLANES - 1, cols] * w[1:2, :]
             + a_s[SUBLANES:ts + SUBLANES, cols] * w[2:3, :]) + fcb_ref[:, cols]
        act_s[:, cols] = (_silu(a) * val).astype(BF16)
    a_s[0:SUBLANES, :] = a_s[ts:ts + SUBLANES, :]
    sub = ts // FFN_DOWN_SPLIT
    for r in range(FFN_DOWN_SPLIT):
        rows = slice(r * sub, (r + 1) * sub)
        ffn = _wdot(act_s[rows, :], wdown_ref[...])
        x2 = x[rows] + gate2 * ffn
        o_ref[rows, :] = _rms(x2, fg_ref[...]) * (1.0 + final_scale) + final_shift


def _ffn_call(x, mod, fmod, g2, w_up, fcw, fcb, w_down, fg):
    s = x.shape[0]
    ts = TS_FFN
    row = lambda i: (i, 0)
    return pl.pallas_call(
        _ffn_kernel,
        out_shape=jax.ShapeDtypeStruct((s, D_MODEL), F32),
        grid=(s // ts,),
        in_specs=[
            pl.BlockSpec((ts, D_MODEL), row),
            _const_spec(mod.shape),
            _const_spec(fmod.shape),
            _const_spec(g2.shape),
            _const_spec(w_up.shape),
            _const_spec(fcw.shape),
            _const_spec(fcb.shape),
            _const_spec(w_down.shape),
            _const_spec(fg.shape),
        ],
        out_specs=pl.BlockSpec((ts, D_MODEL), row),
        scratch_shapes=[
            pltpu.VMEM((ts + SUBLANES, D_FF), F32),
            pltpu.VMEM((ts, D_FF), BF16),
        ],
        compiler_params=pltpu.CompilerParams(
            dimension_semantics=("arbitrary",), vmem_limit_bytes=VMEM_LIMIT_BYTES),
        name="convglu_ffn",
    )(x, mod, fmod, g2, w_up, fcw, fcb, w_down, fg)


def kernel(x, c, positions, mod_w, mod_b, norm1_g, w_in, ret_norm_g, short_conv_w, conv_norm_g,
           w_out, norm2_g, w_up, ffn_conv_w, ffn_conv_b, w_down, final_mod_w, final_mod_b,
           final_norm_g):
    b, s, d = x.shape
    assert (b, d) == (1, D_MODEL) and mod_w.shape[0] == 1 and s % max(TS_MIX, TS_FFN) == 0
    c_col = c.reshape(d, 1)
    mod1 = _mod_call(c_col, mod_w[0], mod_b, 3 * d)

    x2d = x.reshape(s, d)
    x1, w_up_bf, w_down_bf, mod2, fmod = _mix_call(
        x2d, positions, mod1, norm1_g, w_in[0], ret_norm_g, short_conv_w[0], conv_norm_g, w_out[0],
        w_up[0], w_down[0], c_col, mod_w[0], mod_b, final_mod_w, final_mod_b[None, :])
    out = _ffn_call(x1, mod2, fmod, norm2_g, w_up_bf, ffn_conv_w[0], ffn_conv_b, w_down_bf,
                    final_norm_g[None, :])
    return out.reshape(b, s, d)
```

```python
import functools

import numpy as np
import jax
import jax.numpy as jnp
from jax import lax
from jax.experimental import pallas as pl
from jax.experimental.pallas import tpu as pltpu

F32 = jnp.float32
BF16 = jnp.bfloat16

D_MODEL = 1024
RET_WIDTH = 512
RET_HEADS = 8
HEAD_DIM = 64
CONV_WIDTH = 512
GROUP_DIM = 64
CHUNK = 128
D_FF = 2816
ROPE_BASE = 10000.0
EPS = 1e-6
IN_COLS = 4 * RET_WIDTH + 3 * CONV_WIDTH

LANES = 128
SUBLANES = 8
HEAD_PAIRS = RET_WIDTH // LANES
VMEM_LIMIT_BYTES = 56 * 1024 * 1024

TS_MIX = 512
TS_FFN = 512
FFN_COL_CHUNK = 256
FFN_DOWN_SPLIT = 2


def _const_spec(shape):
    return pl.BlockSpec(shape, lambda i: (0,) * len(shape), pipeline_mode=pl.Buffered(1))


def _rms(x, gain):
    ms = jnp.mean(x * x, axis=-1, keepdims=True)
    return x * lax.rsqrt(ms + EPS) * gain


def _silu(x):
    return x * jax.nn.sigmoid(x)


def _wdot(act, w):
    return lax.dot_general(act, w, (((1,), (0,)), ((), ())), preferred_element_type=F32)


MOD_BLOCK = 512


def _mod_matvec(c_col, w, b):
    return jnp.sum(_silu(c_col) * w, axis=0, keepdims=True) + b


def _mod_kernel(c_ref, w_ref, b_ref, o_ref):
    o_ref[...] = _mod_matvec(c_ref[...], w_ref[...], b_ref[...])


def _mod_call(c_col, w, b_row, n_cols):
    d = w.shape[0]
    bn = MOD_BLOCK
    return pl.pallas_call(
        _mod_kernel,
        out_shape=jax.ShapeDtypeStruct((1, n_cols), F32),
        grid=(n_cols // bn,),
        in_specs=[
            pl.BlockSpec((d, 1), lambda j: (0, 0)),
            pl.BlockSpec((d, bn), lambda j: (0, j)),
            pl.BlockSpec((1, bn), lambda j: (0, j)),
        ],
        out_specs=pl.BlockSpec((1, bn), lambda j: (0, j)),
        compiler_params=pltpu.CompilerParams(
            dimension_semantics=("arbitrary",), vmem_limit_bytes=VMEM_LIMIT_BYTES),
        name="adaln_mod",
    )(c_col, w, b_row)


def _group_rms(y, gain):
    lane = lax.broadcasted_iota(jnp.int32, (y.shape[0], LANES), 1)
    group0 = (lane & GROUP_DIM) == 0
    outs = []
    for lo in range(0, y.shape[1], LANES):
        yy = y[:, lo:lo + LANES]
        y2 = yy * yy
        s0 = jnp.sum(jnp.where(group0, y2, 0.0), axis=-1, keepdims=True)
        s1 = jnp.sum(jnp.where(group0, 0.0, y2), axis=-1, keepdims=True)
        ss = jnp.where(group0, s0, s1)
        outs.append(yy * lax.rsqrt(ss * (1.0 / GROUP_DIM) + EPS))
    return jnp.concatenate(outs, axis=1) * gain


def _causal_conv3(tail_ref, w_ref, ts):
    w = w_ref[...]
    y = (tail_ref[SUBLANES - 2:ts + SUBLANES - 2, :] * w[0:1, :]
         + tail_ref[SUBLANES - 1:ts + SUBLANES - 1, :] * w[1:2, :]
         + tail_ref[SUBLANES:ts + SUBLANES, :] * w[2:3, :])
    tail_ref[0:SUBLANES, :] = tail_ref[ts:ts + SUBLANES, :]
    return y


def _mix_kernel(x_ref, pos_ref, mod_ref, g1_ref, win_ref, invf_ref, dmask_ref, xi_ref, zeta_ref,
                rdecay_ref, rmask_ref, retg_ref, scw_ref, cng_ref, wout_ref, wup_ref, wdown_ref,
                c_ref, modw_ref, modb_ref, fmodw_ref, fmodb_ref,
                o_ref, wup_bf_ref, wdown_bf_ref, mod2_ref, fmod_ref, q_s, kt_s, v_s, y_s, u_s, r_s):
    ts = x_ref.shape[0]

    @pl.when(pl.program_id(0) == 0)
    def _():
        r_s[...] = jnp.zeros_like(r_s)
        u_s[0:SUBLANES, :] = jnp.zeros((SUBLANES, CONV_WIDTH), F32)
        mod2_ref[...] = modb_ref[...]
        fmod_ref[...] = fmodb_ref[...]

    x = x_ref[...]
    shift1 = mod_ref[:, 0:D_MODEL]
    scale1 = mod_ref[:, D_MODEL:2 * D_MODEL]
    gate1 = mod_ref[:, 2 * D_MODEL:3 * D_MODEL]
    h = (_rms(x, g1_ref[...]) * (1.0 + scale1) + shift1).astype(BF16)

    def proj(col):
        return _wdot(h, win_ref[:, col:col + RET_WIDTH])

    half = HEAD_DIM // 2
    ang_t = invf_ref[...] * pos_ref[...].astype(F32)
    cos_t = jnp.cos(ang_t)
    sin_t = jnp.sin(ang_t)
    trig = jnp.concatenate([cos_t, sin_t, cos_t, sin_t], axis=0).T
    lane = lax.broadcasted_iota(jnp.int32, (ts, LANES), 1)
    first_half = (lane & half) == 0
    cos = jnp.where(first_half, trig, pltpu.roll(trig, half, 1))
    sin_signed = jnp.where(first_half, -pltpu.roll(trig, LANES - half, 1), trig)

    def rotary(t):
        swapped = jnp.where(first_half, pltpu.roll(t, LANES - half, 1), pltpu.roll(t, half, 1))
        return t * cos + swapped * sin_signed

    q = proj(0)
    k = proj(RET_WIDTH)
    k_rot = []
    for p in range(HEAD_PAIRS):
        ln = slice(p * LANES, (p + 1) * LANES)
        q_s[:, ln] = rotary(q[:, ln]).astype(BF16)
        k_rot.append(rotary(k[:, ln]) * (HEAD_DIM ** -0.5))
    k_t = jnp.concatenate(k_rot, axis=1).T
    dim_head0 = (lax.broadcasted_iota(jnp.int32, k_t.shape, 0) & HEAD_DIM) == 0
    kt_s[0] = k_t.astype(BF16)
    kt_s[1] = jnp.where(dim_head0, k_t, 0.0).astype(BF16)
    kt_s[2] = jnp.where(dim_head0, 0.0, k_t).astype(BF16)
    v = proj(2 * RET_WIDTH)
    lane_head0 = (lax.broadcasted_iota(jnp.int32, v.shape, 1) & HEAD_DIM) == 0
    v_s[0] = v.astype(BF16)
    v_s[1] = jnp.where(lane_head0, v, 0.0).astype(BF16)
    v_s[2] = jnp.where(lane_head0, 0.0, v).astype(BF16)

    for c in range(ts // CHUNK):
        rows = slice(c * CHUNK, (c + 1) * CHUNK)
        for p in range(HEAD_PAIRS):
            ln = slice(p * LANES, (p + 1) * LANES)
            qc = q_s[rows, ln]
            kcat = jnp.concatenate([kt_s[1, ln, rows], kt_s[2, ln, rows]], axis=1)
            scores = jnp.dot(qc, kcat, preferred_element_type=F32) * dmask_ref[p]
            vcat = jnp.concatenate([v_s[1, rows, ln], v_s[2, rows, ln]], axis=0)
            y_inner = jnp.dot(scores.astype(BF16), vcat, preferred_element_type=F32)
            r = r_s[p]
            y_cross = jnp.dot(qc, r.astype(BF16), preferred_element_type=F32) * xi_ref[p]
            y_s[rows, ln] = y_inner + y_cross
            vz = (v_s[0, rows, ln].astype(F32) * zeta_ref[p]).astype(BF16)
            kv = jnp.dot(kt_s[0, ln, rows], vz, preferred_element_type=F32)
            r_s[p] = r * rdecay_ref[p] + kv * rmask_ref[p]

    y_ret = _group_rms(y_s[...], retg_ref[...]) * _silu(proj(3 * RET_WIDTH))

    u_s[SUBLANES:ts + SUBLANES, :] = proj(4 * RET_WIDTH + CONV_WIDTH) * proj(4 * RET_WIDTH + 2 * CONV_WIDTH)
    y_conv = proj(4 * RET_WIDTH) * _causal_conv3(u_s, scw_ref, ts)
    y_conv = _group_rms(y_conv, cng_ref[...])

    cat = jnp.concatenate([y_ret.astype(BF16), y_conv.astype(BF16)], axis=1)
    mixed = _wdot(cat, wout_ref[...])
    o_ref[...] = x + gate1 * mixed

    wup_bf_ref[...] = wup_ref[...].astype(BF16)
    wdown_bf_ref[...] = wdown_ref[...].astype(BF16)
    ca = _silu(c_ref[...])
    mod2_ref[...] += jnp.sum(ca * modw_ref[...], axis=0, keepdims=True)
    fmod_ref[...] += jnp.sum(ca * fmodw_ref[...], axis=0, keepdims=True)


def _retention_tables():
    h = np.arange(RET_HEADS, dtype=np.float64)
    log_gamma = np.log1p(-np.exp2(-5.0 - h))
    idx = np.arange(CHUNK, dtype=np.float64)
    rel = idx[:, None] - idx[None, :]
    dmask = np.where(rel[None] >= 0.0,
                     np.exp(np.maximum(rel, 0.0)[None] * log_gamma[:, None, None]), 0.0)
    dmask = dmask.reshape(HEAD_PAIRS, 2, CHUNK, CHUNK).transpose(0, 2, 1, 3)
    dmask = dmask.reshape(HEAD_PAIRS, CHUNK, 2 * CHUNK)
    zeta = np.exp((CHUNK - 1.0 - idx)[None, :] * log_gamma[:, None])
    xi = np.exp((idx + 1.0)[None, :] * log_gamma[:, None])
    chunk_decay = np.exp(CHUNK * log_gamma)

    def per_lane(t):
        t = t.reshape(HEAD_PAIRS, 2, CHUNK)
        return np.repeat(np.transpose(t, (0, 2, 1)), HEAD_DIM, axis=2)

    lane_head = np.arange(LANES) // HEAD_DIM
    rmask = np.broadcast_to(lane_head[:, None] == lane_head[None, :], (HEAD_PAIRS, LANES, LANES))
    rdecay = np.repeat(chunk_decay.reshape(HEAD_PAIRS, 1, 2), HEAD_DIM, axis=2)
    rdecay = np.broadcast_to(rdecay, (HEAD_PAIRS, LANES, LANES))
    return tuple(jnp.asarray(np.ascontiguousarray(t), dtype=F32)
                 for t in (dmask, per_lane(xi), per_lane(zeta), rdecay, rmask))


def _side_cast_spec(w, n_steps):
    k, n = w.shape
    nblk = next(b for b in range(n_steps, 0, -1)
                if k % b == 0 and (k // b) % (2 * SUBLANES) == 0)
    return pl.BlockSpec((k // nblk, n), lambda i: (jnp.minimum(i, nblk - 1), 0))


def _mix_call(x, pos_row, mod1, g1, w_in, retg, scw, cng, w_out, w_up, w_down,
              c_col, mod_w, mod_b, fmod_w, fmod_b):
    s = x.shape[0]
    ts = TS_MIX
    n_steps = s // ts
    half = HEAD_DIM // 2
    invf = (ROPE_BASE ** (-jnp.arange(half, dtype=F32) / half))[:, None]
    dmask, xi, zeta, rdecay, rmask = _retention_tables()
    row = lambda i: (i, 0)
    wup_spec = _side_cast_spec(w_up, n_steps)
    wdown_spec = _side_cast_spec(w_down, n_steps)
    n_mod2 = mod1.shape[1]
    assert mod_w.shape[1] == 2 * n_mod2
    n_fmod = fmod_w.shape[1]
    c_rows = c_col.shape[0] // n_steps
    return pl.pallas_call(
        _mix_kernel,
        out_shape=(jax.ShapeDtypeStruct((s, D_MODEL), F32),
                   jax.ShapeDtypeStruct(w_up.shape, BF16),
                   jax.ShapeDtypeStruct(w_down.shape, BF16),
                   jax.ShapeDtypeStruct((1, n_mod2), F32),
                   jax.ShapeDtypeStruct((1, n_fmod), F32)),
        grid=(n_steps,),
        in_specs=[
            pl.BlockSpec((ts, D_MODEL), row),
            pl.BlockSpec((1, ts), lambda i: (0, i)),
            _const_spec(mod1.shape),
            _const_spec(g1.shape),
            _const_spec(w_in.shape),
            _const_spec(invf.shape),
            _const_spec(dmask.shape),
            _const_spec(xi.shape),
            _const_spec(zeta.shape),
            _const_spec(rdecay.shape),
            _const_spec(rmask.shape),
            _const_spec(retg.shape),
            _const_spec(scw.shape),
            _const_spec(cng.shape),
            _const_spec(w_out.shape),
            wup_spec,
            wdown_spec,
            pl.BlockSpec((c_rows, 1), row),
            pl.BlockSpec((c_rows, n_mod2), lambda i: (i, 1)),
            pl.BlockSpec((1, n_mod2), lambda i: (0, 1)),
            pl.BlockSpec((c_rows, n_fmod), row),
            pl.BlockSpec((1, n_fmod), lambda i: (0, 0)),
        ],
        out_specs=(pl.BlockSpec((ts, D_MODEL), row), wup_spec, wdown_spec,
                   pl.BlockSpec((1, n_mod2), lambda i: (0, 0)),
                   pl.BlockSpec((1, n_fmod), lambda i: (0, 0))),
        scratch_shapes=[
            pltpu.VMEM((ts, RET_WIDTH), BF16),
            pltpu.VMEM((3, RET_WIDTH, ts), BF16),
            pltpu.VMEM((3, ts, RET_WIDTH), BF16),
            pltpu.VMEM((ts, RET_WIDTH), F32),
            pltpu.VMEM((ts + SUBLANES, CONV_WIDTH), F32),
            pltpu.VMEM((HEAD_PAIRS, LANES, LANES), F32),
        ],
        compiler_params=pltpu.CompilerParams(
            dimension_semantics=("arbitrary",), vmem_limit_bytes=VMEM_LIMIT_BYTES),
        name="token_mix",
    )(x, pos_row, mod1, g1, w_in, invf, dmask, xi, zeta, rdecay, rmask, retg, scw, cng, w_out,
      w_up, w_down, c_col, mod_w, mod_b, fmod_w, fmod_b)


def _ffn_kernel(x_ref, mod_ref, fmod_ref, g2_ref, wup_ref, fcw_ref, fcb_ref, wdown_ref, fg_ref,
                o_ref, a_s, act_s):
    ts = x_ref.shape[0]

    @pl.when(pl.program_id(0) == 0)
    def _():
        a_s[0:SUBLANES, :] = jnp.zeros((SUBLANES, D_FF), F32)

    shift2 = mod_ref[:, 0:D_MODEL]
    scale2 = mod_ref[:, D_MODEL:2 * D_MODEL]
    gate2 = mod_ref[:, 2 * D_MODEL:3 * D_MODEL]
    final_shift = fmod_ref[:, 0:D_MODEL]
    final_scale = fmod_ref[:, D_MODEL:2 * D_MODEL]
    x = x_ref[...]
    h = (_rms(x, g2_ref[...]) * (1.0 + scale2) + shift2).astype(BF16)
    for lo in range(0, D_FF, FFN_COL_CHUNK):
        cols = slice(lo, lo + FFN_COL_CHUNK)
        a_s[SUBLANES:ts + SUBLANES, cols] = _wdot(h, wup_ref[:, lo:lo + FFN_COL_CHUNK])
        val = _wdot(h, wup_ref[:, D_FF + lo:D_FF + lo + FFN_COL_CHUNK])
        w = fcw_ref[:, cols]
        a = (a_s[SUBLANES - 2:ts + SUBLANES - 2, cols] * w[0:1, :]
             + a_s[SUBLANES - 1:ts + SUBLANES - 1, cols] * w[1:2, :]
             + a_s[SUBLANES:ts + SUBLANES, cols] * w[2:3, :]) + fcb_ref[:, cols]
        act_s[:, cols] = (_silu(a) * val).astype(BF16)
    a_s[0:SUBLANES, :] = a_s[ts:ts + SUBLANES, :]
    sub = ts // FFN_DOWN_SPLIT
    for r in range(FFN_DOWN_SPLIT):
        rows = slice(r * sub, (r + 1) * sub)
        ffn = _wdot(act_s[rows, :], wdown_ref[...])
        x2 = x[rows] + gate2 * ffn
        o_ref[rows, :] = _rms(x2, fg_ref[...]) * (1.0 + final_scale) + final_shift


def _ffn_call(x, mod, fmod, g2, w_up, fcw, fcb, w_down, fg):
    s = x.shape[0]
    ts = TS_FFN
    row = lambda i: (i, 0)
    return pl.pallas_call(
        _ffn_kernel,
        out_shape=jax.ShapeDtypeStruct((s, D_MODEL), F32),
        grid=(s // ts,),
        in_specs=[
            pl.BlockSpec((ts, D_MODEL), row),
            _const_spec(mod.shape),
            _const_spec(fmod.shape),
            _const_spec(g2.shape),
            _const_spec(w_up.shape),
            _const_spec(fcw.shape),
            _const_spec(fcb.shape),
            _const_spec(w_down.shape),
            _const_spec(fg.shape),
        ],
        out_specs=pl.BlockSpec((ts, D_MODEL), row),
        scratch_shapes=[
            pltpu.VMEM((ts + SUBLANES, D_FF), F32),
            pltpu.VMEM((ts, D_FF), BF16),
        ],
        compiler_params=pltpu.CompilerParams(
            dimension_semantics=("arbitrary",), vmem_limit_bytes=VMEM_LIMIT_BYTES),
        name="convglu_ffn",
    )(x, mod, fmod, g2, w_up, fcw, fcb, w_down, fg)


def kernel(x, c, positions, mod_w, mod_b, norm1_g, w_in, ret_norm_g, short_conv_w, conv_norm_g,
           w_out, norm2_g, w_up, ffn_conv_w, ffn_conv_b, w_down, final_mod_w, final_mod_b,
           final_norm_g):
    b, s, d = x.shape
    assert (b, d) == (1, D_MODEL) and mod_w.shape[0] == 1 and s % max(TS_MIX, TS_FFN) == 0
    c_col = c.reshape(d, 1)
    mod1 = _mod_call(c_col, mod_w[0], mod_b, 3 * d)

    x2d = x.reshape(s, d)
    x1, w_up_bf, w_down_bf, mod2, fmod = _mix_call(
        x2d, positions, mod1, norm1_g, w_in[0], ret_norm_g, short_conv_w[0], conv_norm_g, w_out[0],
        w_up[0], w_down[0], c_col, mod_w[0], mod_b, final_mod_w, final_mod_b[None, :])
    out = _ffn_call(x1, mod2, fmod, norm2_g, w_up_bf, ffn_conv_w[0], ffn_conv_b, w_down_bf,
                    final_norm_g[None, :])
    return out.reshape(b, s, d)
```

```python
import functools

import numpy as np
import jax
import jax.numpy as jnp
from jax import lax
from jax.experimental import pallas as pl
from jax.experimental.pallas import tpu as pltpu

F32 = jnp.float32
BF16 = jnp.bfloat16

D_MODEL = 1024
RET_WIDTH = 512
RET_HEADS = 8
HEAD_DIM = 64
CONV_WIDTH = 512
GROUP_DIM = 64
CHUNK = 128
D_FF = 2816
ROPE_BASE = 10000.0
EPS = 1e-6
IN_COLS = 4 * RET_WIDTH + 3 * CONV_WIDTH

LANES = 128
SUBLANES = 8
HEAD_PAIRS = RET_WIDTH // LANES
VMEM_LIMIT_BYTES = 56 * 1024 * 1024

TS_MIX = 512
MIX_OUT_SPLIT = 2
TS_FFN = 512
FFN_COL_CHUNK = 256
FFN_DOWN_SPLIT = 2


def _const_spec(shape):
    return pl.BlockSpec(shape, lambda i: (0,) * len(shape), pipeline_mode=pl.Buffered(1))


def _rms(x, gain):
    ms = jnp.mean(x * x, axis=-1, keepdims=True)
    return x * lax.rsqrt(ms + EPS) * gain


def _silu(x):
    return x * jax.nn.sigmoid(x)


def _wdot(act, w):
    return lax.dot_general(act, w, (((1,), (0,)), ((), ())), preferred_element_type=F32)


MOD_BLOCK = 512


def _mod_kernel(c_ref, w_ref, b_ref, o_ref):
    o_ref[...] = jnp.sum(_silu(c_ref[...]) * w_ref[...], axis=0, keepdims=True) + b_ref[...]


def _mod_call(c_col, w, b_row, n_cols):
    d = w.shape[0]
    bn = MOD_BLOCK
    return pl.pallas_call(
        _mod_kernel,
        out_shape=jax.ShapeDtypeStruct((1, n_cols), F32),
        grid=(n_cols // bn,),
        in_specs=[
            pl.BlockSpec((d, 1), lambda j: (0, 0)),
            pl.BlockSpec((d, bn), lambda j: (0, j)),
            pl.BlockSpec((1, bn), lambda j: (0, j)),
        ],
        out_specs=pl.BlockSpec((1, bn), lambda j: (0, j)),
        compiler_params=pltpu.CompilerParams(
            dimension_semantics=("arbitrary",), vmem_limit_bytes=VMEM_LIMIT_BYTES),
        name="adaln_mod",
    )(c_col, w, b_row)


def _group_rms(y, gain):
    lane = lax.broadcasted_iota(jnp.int32, (y.shape[0], LANES), 1)
    group0 = (lane & GROUP_DIM) == 0
    outs = []
    for lo in range(0, y.shape[1], LANES):
        yy = y[:, lo:lo + LANES]
        y2 = yy * yy
        s0 = jnp.sum(jnp.where(group0, y2, 0.0), axis=-1, keepdims=True)
        s1 = jnp.sum(jnp.where(group0, 0.0, y2), axis=-1, keepdims=True)
        ss = jnp.where(group0, s0, s1)
        outs.append(yy * lax.rsqrt(ss * (1.0 / GROUP_DIM) + EPS))
    return jnp.concatenate(outs, axis=1) * gain


def _causal_conv3(tail_ref, w_ref, ts):
    w = w_ref[...]
    y = (tail_ref[SUBLANES - 2:ts + SUBLANES - 2, :] * w[0:1, :]
         + tail_ref[SUBLANES - 1:ts + SUBLANES - 1, :] * w[1:2, :]
         + tail_ref[SUBLANES:ts + SUBLANES, :] * w[2:3, :])
    tail_ref[0:SUBLANES, :] = tail_ref[ts:ts + SUBLANES, :]
    return y


def _mix_kernel(x_ref, xprev_ref, pos_ref, mod_ref, g1_ref, win_ref, invf_ref, dmask_ref, xi_ref,
                zeta_ref, rdecay_ref, rmask_ref, retg_ref, scw_ref, cng_ref, wout_ref, wup_ref,
                wdown_ref, c_ref, modw_ref, modb_ref, fmodw_ref, fmodb_ref,
                o_ref, wup_bf_ref, wdown_bf_ref, mod2_ref, fmod_ref,
                q_s, kt_s, v_s, y_s, u_s, r_s, cat_s, *, n_tiles):
    ts = x_ref.shape[0]
    step = pl.program_id(0)

    @pl.when(step == 0)
    def _():
        r_s[...] = jnp.zeros_like(r_s)
        u_s[0:SUBLANES, :] = jnp.zeros((SUBLANES, CONV_WIDTH), F32)
        cat_s[...] = jnp.zeros_like(cat_s)
        mod2_ref[...] = modb_ref[...]
        fmod_ref[...] = fmodb_ref[...]

    shift1 = mod_ref[:, 0:D_MODEL]
    scale1 = mod_ref[:, D_MODEL:2 * D_MODEL]
    gate1 = mod_ref[:, 2 * D_MODEL:3 * D_MODEL]

    wup_bf_ref[...] = wup_ref[...].astype(BF16)
    wdown_bf_ref[...] = wdown_ref[...].astype(BF16)
    ca = _silu(c_ref[...]) * jnp.where(step < n_tiles, 1.0, 0.0)
    mod2_ref[...] += jnp.sum(ca * modw_ref[...], axis=0, keepdims=True)
    fmod_ref[...] += jnp.sum(ca * fmodw_ref[...], axis=0, keepdims=True)

    sub = ts // MIX_OUT_SPLIT
    for r in range(MIX_OUT_SPLIT):
        rows = slice(r * sub, (r + 1) * sub)
        mixed = _wdot(cat_s[rows, :], wout_ref[...])
        o_ref[rows, :] = xprev_ref[rows, :] + gate1 * mixed

    x = x_ref[...]
    h = (_rms(x, g1_ref[...]) * (1.0 + scale1) + shift1).astype(BF16)

    def proj(col):
        return _wdot(h, win_ref[:, col:col + RET_WIDTH])

    u_s[SUBLANES:ts + SUBLANES, :] = proj(4 * RET_WIDTH + CONV_WIDTH) * proj(4 * RET_WIDTH + 2 * CONV_WIDTH)
    y_conv = proj(4 * RET_WIDTH) * _causal_conv3(u_s, scw_ref, ts)
    cat_s[:, RET_WIDTH:] = _group_rms(y_conv, cng_ref[...]).astype(BF16)

    half = HEAD_DIM // 2
    ang_t = invf_ref[...] * pos_ref[...].astype(F32)
    cos_t = jnp.cos(ang_t)
    sin_t = jnp.sin(ang_t)
    trig = jnp.concatenate([cos_t, sin_t, cos_t, sin_t], axis=0).T
    lane = lax.broadcasted_iota(jnp.int32, (ts, LANES), 1)
    first_half = (lane & half) == 0
    cos = jnp.where(first_half, trig, pltpu.roll(trig, half, 1))
    sin_signed = jnp.where(first_half, -pltpu.roll(trig, LANES - half, 1), trig)

    def rotary(t):
        swapped = jnp.where(first_half, pltpu.roll(t, LANES - half, 1), pltpu.roll(t, half, 1))
        return t * cos + swapped * sin_signed

    q = proj(0)
    k = proj(RET_WIDTH)
    k_rot = []
    for p in range(HEAD_PAIRS):
        ln = slice(p * LANES, (p + 1) * LANES)
        q_s[:, ln] = rotary(q[:, ln]).astype(BF16)
        k_rot.append(rotary(k[:, ln]) * (HEAD_DIM ** -0.5))
    k_t = jnp.concatenate(k_rot, axis=1).T
    dim_head0 = (lax.broadcasted_iota(jnp.int32, k_t.shape, 0) & HEAD_DIM) == 0
    kt_s[0] = k_t.astype(BF16)
    kt_s[1] = jnp.where(dim_head0, k_t, 0.0).astype(BF16)
    kt_s[2] = jnp.where(dim_head0, 0.0, k_t).astype(BF16)
    v = proj(2 * RET_WIDTH)
    lane_head0 = (lax.broadcasted_iota(jnp.int32, v.shape, 1) & HEAD_DIM) == 0
    v_s[0] = v.astype(BF16)
    v_s[1] = jnp.where(lane_head0, v, 0.0).astype(BF16)
    v_s[2] = jnp.where(lane_head0, 0.0, v).astype(BF16)

    for c in range(ts // CHUNK):
        rows = slice(c * CHUNK, (c + 1) * CHUNK)
        for p in range(HEAD_PAIRS):
            ln = slice(p * LANES, (p + 1) * LANES)
            qc = q_s[rows, ln]
            kcat = jnp.concatenate([kt_s[1, ln, rows], kt_s[2, ln, rows]], axis=1)
            scores = jnp.dot(qc, kcat, preferred_element_type=F32) * dmask_ref[p]
            vcat = jnp.concatenate([v_s[1, rows, ln], v_s[2, rows, ln]], axis=0)
            y_inner = jnp.dot(scores.astype(BF16), vcat, preferred_element_type=F32)
            r = r_s[p]
            y_cross = jnp.dot(qc, r.astype(BF16), preferred_element_type=F32) * xi_ref[p]
            y_s[rows, ln] = y_inner + y_cross
            vz = (v_s[0, rows, ln].astype(F32) * zeta_ref[p]).astype(BF16)
            kv = jnp.dot(kt_s[0, ln, rows], vz, preferred_element_type=F32)
            r_s[p] = r * rdecay_ref[p] + kv * rmask_ref[p]

    y_norm = _group_rms(y_s[...], retg_ref[...])
    cat_s[:, :RET_WIDTH] = (y_norm * _silu(proj(3 * RET_WIDTH))).astype(BF16)


def _retention_tables():
    h = np.arange(RET_HEADS, dtype=np.float64)
    log_gamma = np.log1p(-np.exp2(-5.0 - h))
    idx = np.arange(CHUNK, dtype=np.float64)
    rel = idx[:, None] - idx[None, :]
    dmask = np.where(rel[None] >= 0.0,
                     np.exp(np.maximum(rel, 0.0)[None] * log_gamma[:, None, None]), 0.0)
    dmask = dmask.reshape(HEAD_PAIRS, 2, CHUNK, CHUNK).transpose(0, 2, 1, 3)
    dmask = dmask.reshape(HEAD_PAIRS, CHUNK, 2 * CHUNK)
    zeta = np.exp((CHUNK - 1.0 - idx)[None, :] * log_gamma[:, None])
    xi = np.exp((idx + 1.0)[None, :] * log_gamma[:, None])
    chunk_decay = np.exp(CHUNK * log_gamma)

    def per_lane(t):
        t = t.reshape(HEAD_PAIRS, 2, CHUNK)
        return np.repeat(np.transpose(t, (0, 2, 1)), HEAD_DIM, axis=2)

    lane_head = np.arange(LANES) // HEAD_DIM
    rmask = np.broadcast_to(lane_head[:, None] == lane_head[None, :], (HEAD_PAIRS, LANES, LANES))
    rdecay = np.repeat(chunk_decay.reshape(HEAD_PAIRS, 1, 2), HEAD_DIM, axis=2)
    rdecay = np.broadcast_to(rdecay, (HEAD_PAIRS, LANES, LANES))
    return tuple(jnp.asarray(np.ascontiguousarray(t), dtype=F32)
                 for t in (dmask, per_lane(xi), per_lane(zeta), rdecay, rmask))


def _side_cast_spec(w, n_steps):
    k, n = w.shape
    nblk = next(b for b in range(n_steps, 0, -1)
                if k % b == 0 and (k // b) % (2 * SUBLANES) == 0)
    return pl.BlockSpec((k // nblk, n), lambda i: (jnp.minimum(i, nblk - 1), 0))


def _mix_call(x, pos_row, mod1, g1, w_in, retg, scw, cng, w_out, w_up, w_down,
              c_col, mod_w, mod_b, fmod_w, fmod_b):
    s = x.shape[0]
    ts = TS_MIX
    n_tiles = s // ts
    half = HEAD_DIM // 2
    invf = (ROPE_BASE ** (-jnp.arange(half, dtype=F32) / half))[:, None]
    dmask, xi, zeta, rdecay, rmask = _retention_tables()
    cur = lambda i: (jnp.minimum(i, n_tiles - 1), 0)
    prev = lambda i: (jnp.maximum(i - 1, 0), 0)
    wup_spec = _side_cast_spec(w_up, n_tiles)
    wdown_spec = _side_cast_spec(w_down, n_tiles)
    n_mod2 = mod1.shape[1]
    assert mod_w.shape[1] == 2 * n_mod2
    n_fmod = fmod_w.shape[1]
    c_rows = c_col.shape[0] // n_tiles
    return pl.pallas_call(
        functools.partial(_mix_kernel, n_tiles=n_tiles),
        out_shape=(jax.ShapeDtypeStruct((s, D_MODEL), F32),
                   jax.ShapeDtypeStruct(w_up.shape, BF16),
                   jax.ShapeDtypeStruct(w_down.shape, BF16),
                   jax.ShapeDtypeStruct((1, n_mod2), F32),
                   jax.ShapeDtypeStruct((1, n_fmod), F32)),
        grid=(n_tiles + 1,),
        in_specs=[
            pl.BlockSpec((ts, D_MODEL), cur),
            pl.BlockSpec((ts, D_MODEL), prev),
            pl.BlockSpec((1, ts), lambda i: (0, jnp.minimum(i, n_tiles - 1))),
            _const_spec(mod1.shape),
            _const_spec(g1.shape),
            _const_spec(w_in.shape),
            _const_spec(invf.shape),
            _const_spec(dmask.shape),
            _const_spec(xi.shape),
            _const_spec(zeta.shape),
            _const_spec(rdecay.shape),
            _const_spec(rmask.shape),
            _const_spec(retg.shape),
            _const_spec(scw.shape),
            _const_spec(cng.shape),
            _const_spec(w_out.shape),
            wup_spec,
            wdown_spec,
            pl.BlockSpec((c_rows, 1), cur),
            pl.BlockSpec((c_rows, n_mod2), lambda i: (jnp.minimum(i, n_tiles - 1), 1)),
            pl.BlockSpec((1, n_mod2), lambda i: (0, 1)),
            pl.BlockSpec((c_rows, n_fmod), cur),
            pl.BlockSpec((1, n_fmod), lambda i: (0, 0)),
        ],
        out_specs=(pl.BlockSpec((ts, D_MODEL), prev), wup_spec, wdown_spec,
                   pl.BlockSpec((1, n_mod2), lambda i: (0, 0)),
                   pl.BlockSpec((1, n_fmod), lambda i: (0, 0))),
        scratch_shapes=[
            pltpu.VMEM((ts, RET_WIDTH), BF16),
            pltpu.VMEM((3, RET_WIDTH, ts), BF16),
            pltpu.VMEM((3, ts, RET_WIDTH), BF16),
            pltpu.VMEM((ts, RET_WIDTH), F32),
            pltpu.VMEM((ts + SUBLANES, CONV_WIDTH), F32),
            pltpu.VMEM((HEAD_PAIRS, LANES, LANES), F32),
            pltpu.VMEM((ts, 2 * RET_WIDTH), BF16),
        ],
        compiler_params=pltpu.CompilerParams(
            dimension_semantics=("arbitrary",), vmem_limit_bytes=VMEM_LIMIT_BYTES),
        name="token_mix",
    )(x, x, pos_row, mod1, g1, w_in, invf, dmask, xi, zeta, rdecay, rmask, retg, scw, cng, w_out,
      w_up, w_down, c_col, mod_w, mod_b, fmod_w, fmod_b)


def _ffn_kernel(x_ref, mod_ref, fmod_ref, g2_ref, wup_ref, fcw_ref, fcb_ref, wdown_ref, fg_ref,
                o_ref, a_s, act_s):
    ts = x_ref.shape[0]

    @pl.when(pl.program_id(0) == 0)
    def _():
        a_s[0:SUBLANES, :] = jnp.zeros((SUBLANES, D_FF), F32)

    shift2 = mod_ref[:, 0:D_MODEL]
    scale2 = mod_ref[:, D_MODEL:2 * D_MODEL]
    gate2 = mod_ref[:, 2 * D_MODEL:3 * D_MODEL]
    final_shift = fmod_ref[:, 0:D_MODEL]
    final_scale = fmod_ref[:, D_MODEL:2 * D_MODEL]
    x = x_ref[...]
    h = (_rms(x, g2_ref[...]) * (1.0 + scale2) + shift2).astype(BF16)
    for lo in range(0, D_FF, FFN_COL_CHUNK):
        cols = slice(lo, lo + FFN_COL_CHUNK)
        a_s[SUBLANES:ts + SUBLANES, cols] = _wdot(h, wup_ref[:, lo:lo + FFN_COL_CHUNK])
        val = _wdot(h, wup_ref[:, D_FF + lo:D_FF + lo + FFN_COL_CHUNK])
        w = fcw_ref[:, cols]
        a = (a_s[SUBLANES - 2:ts + SUBLANES - 2, cols] * w[0:1, :]
             + a_s[SUBLANES - 1:ts + SUBLANES - 1, cols] * w[1:2, :]
             + a_s[SUBLANES:ts + SUBLANES, cols] * w[2:3, :]) + fcb_ref[:, cols]
        act_s[:, cols] = (_silu(a) * val).astype(BF16)
    a_s[0:SUBLANES, :] = a_s[ts:ts + SUBLANES, :]
    sub = ts // FFN_DOWN_SPLIT
    for r in range(FFN_DOWN_SPLIT):
        rows = slice(r * sub, (r + 1) * sub)
        ffn = _wdot(act_s[rows, :], wdown_ref[...])
        x2 = x[rows] + gate2 * ffn
        o_ref[rows, :] = _rms(x2, fg_ref[...]) * (1.0 + final_scale) + final_shift


def _ffn_call(x, mod, fmod, g2, w_up, fcw, fcb, w_down, fg):
    s = x.shape[0]
    ts = TS_FFN
    row = lambda i: (i, 0)
    return pl.pallas_call(
        _ffn_kernel,
        out_shape=jax.ShapeDtypeStruct((s, D_MODEL), F32),
        grid=(s // ts,),
        in_specs=[
            pl.BlockSpec((ts, D_MODEL), row),
            _const_spec(mod.shape),
            _const_spec(fmod.shape),
            _const_spec(g2.shape),
            _const_spec(w_up.shape),
            _const_spec(fcw.shape),
            _const_spec(fcb.shape),
            _const_spec(w_down.shape),
            _const_spec(fg.shape),
        ],
        out_specs=pl.BlockSpec((ts, D_MODEL), row),
        scratch_shapes=[
            pltpu.VMEM((ts + SUBLANES, D_FF), F32),
            pltpu.VMEM((ts, D_FF), BF16),
        ],
        compiler_params=pltpu.CompilerParams(
            dimension_semantics=("arbitrary",), vmem_limit_bytes=VMEM_LIMIT_BYTES),
        name="convglu_ffn",
    )(x, mod, fmod, g2, w_up, fcw, fcb, w_down, fg)


def kernel(x, c, positions, mod_w, mod_b, norm1_g, w_in, ret_norm_g, short_conv_w, conv_norm_g,
           w_out, norm2_g, w_up, ffn_conv_w, ffn_conv_b, w_down, final_mod_w, final_mod_b,
           final_norm_g):
    b, s, d = x.shape
    assert (b, d) == (1, D_MODEL) and mod_w.shape[0] == 1 and s % max(TS_MIX, TS_FFN) == 0
    c_col = c.reshape(d, 1)
    mod1 = _mod_call(c_col, mod_w[0], mod_b, 3 * d)

    x2d = x.reshape(s, d)
    x1, w_up_bf, w_down_bf, mod2, fmod = _mix_call(
        x2d, positions, mod1, norm1_g, w_in[0], ret_norm_g, short_conv_w[0], conv_norm_g, w_out[0],
        w_up[0], w_down[0], c_col, mod_w[0], mod_b, final_mod_w, final_mod_b[None, :])
    out = _ffn_call(x1, mod2, fmod, norm2_g, w_up_bf, ffn_conv_w[0], ffn_conv_b, w_down_bf,
                    final_norm_g[None, :])
    return out.reshape(b, s, d)
```

```python
import numpy as np
import jax
import jax.numpy as jnp
from jax import lax
from jax.experimental import pallas as pl
from jax.experimental.pallas import tpu as pltpu

F32 = jnp.float32
BF16 = jnp.bfloat16

D_MODEL = 1024
RET_WIDTH = 512
RET_HEADS = 8
HEAD_DIM = 64
CONV_WIDTH = 512
GROUP_DIM = 64
CHUNK = 128
D_FF = 2816
ROPE_BASE = 10000.0
EPS = 1e-6
IN_COLS = 4 * RET_WIDTH + 3 * CONV_WIDTH

LANES = 128
SUBLANES = 8
HEAD_PAIRS = RET_WIDTH // LANES
VMEM_LIMIT_BYTES = 56 * 1024 * 1024

TS_MIX = 512
TS_FFN = 512
FFN_COL_CHUNK = 256
FFN_DOWN_SPLIT = 2


def _const_spec(shape):
    return pl.BlockSpec(shape, lambda i: (0,) * len(shape), pipeline_mode=pl.Buffered(1))


def _rms(x, gain):
    ms = jnp.mean(x * x, axis=-1, keepdims=True)
    return x * lax.rsqrt(ms + EPS) * gain


def _silu(x):
    return x * jax.nn.sigmoid(x)


def _wdot(act, w):
    return lax.dot_general(act, w, (((1,), (0,)), ((), ())), preferred_element_type=F32)


MOD_BLOCK = 512


def _mod_kernel(c_ref, w_ref, b_ref, o_ref):
    o_ref[...] = jnp.sum(_silu(c_ref[...]) * w_ref[...], axis=0, keepdims=True) + b_ref[...]


def _mod_call(c_col, w, b_row, n_cols):
    d = w.shape[0]
    bn = MOD_BLOCK
    return pl.pallas_call(
        _mod_kernel,
        out_shape=jax.ShapeDtypeStruct((1, n_cols), F32),
        grid=(n_cols // bn,),
        in_specs=[
            pl.BlockSpec((d, 1), lambda j: (0, 0)),
            pl.BlockSpec((d, bn), lambda j: (0, j)),
            pl.BlockSpec((1, bn), lambda j: (0, j)),
        ],
        out_specs=pl.BlockSpec((1, bn), lambda j: (0, j)),
        compiler_params=pltpu.CompilerParams(
            dimension_semantics=("arbitrary",), vmem_limit_bytes=VMEM_LIMIT_BYTES),
        name="adaln_mod",
    )(c_col, w, b_row)


def _group_rms(y, gain):
    lane = lax.broadcasted_iota(jnp.int32, (y.shape[0], LANES), 1)
    group0 = (lane & GROUP_DIM) == 0
    outs = []
    for lo in range(0, y.shape[1], LANES):
        yy = y[:, lo:lo + LANES]
        y2 = yy * yy
        s0 = jnp.sum(jnp.where(group0, y2, 0.0), axis=-1, keepdims=True)
        s1 = jnp.sum(jnp.where(group0, 0.0, y2), axis=-1, keepdims=True)
        ss = jnp.where(group0, s0, s1)
        outs.append(yy * lax.rsqrt(ss * (1.0 / GROUP_DIM) + EPS))
    return jnp.concatenate(outs, axis=1) * gain


def _causal_conv3(tail_ref, w_ref, ts):
    w = w_ref[...]
    y = (tail_ref[SUBLANES - 2:ts + SUBLANES - 2, :] * w[0:1, :]
         + tail_ref[SUBLANES - 1:ts + SUBLANES - 1, :] * w[1:2, :]
         + tail_ref[SUBLANES:ts + SUBLANES, :] * w[2:3, :])
    tail_ref[0:SUBLANES, :] = tail_ref[ts:ts + SUBLANES, :]
    return y


def _mix_kernel(x_ref, pos_ref, mod_ref, g1_ref, win_ref, invf_ref, dmask_ref, xi_ref, zeta_ref,
                rdecay_ref, rmask_ref, retg_ref, scw_ref, cng_ref, wout_ref, wup_ref, wdown_ref,
                c_ref, modw_ref, modb_ref, fmodw_ref, fmodb_ref,
                o_ref, wup_bf_ref, wdown_bf_ref, mod2_ref, fmod_ref, q_s, kt_s, v_s, y_s, u_s, r_s,
                win_s, wout_s):
    ts = x_ref.shape[0]

    @pl.when(pl.program_id(0) == 0)
    def _():
        r_s[...] = jnp.zeros_like(r_s)
        u_s[0:SUBLANES, :] = jnp.zeros((SUBLANES, CONV_WIDTH), F32)
        mod2_ref[...] = modb_ref[...]
        fmod_ref[...] = fmodb_ref[...]
        for col in range(0, IN_COLS, RET_WIDTH):
            win_s[:, col:col + RET_WIDTH] = win_ref[:, col:col + RET_WIDTH].astype(BF16)
        wout_s[...] = wout_ref[...].astype(BF16)

    x = x_ref[...]
    shift1 = mod_ref[:, 0:D_MODEL]
    scale1 = mod_ref[:, D_MODEL:2 * D_MODEL]
    gate1 = mod_ref[:, 2 * D_MODEL:3 * D_MODEL]
    h = (_rms(x, g1_ref[...]) * (1.0 + scale1) + shift1).astype(BF16)

    def proj(col):
        return _wdot(h, win_s[:, col:col + RET_WIDTH])

    half = HEAD_DIM // 2
    ang_t = invf_ref[...] * pos_ref[...].astype(F32)
    cos_t = jnp.cos(ang_t)
    sin_t = jnp.sin(ang_t)
    trig = jnp.concatenate([cos_t, sin_t, cos_t, sin_t], axis=0).T
    lane = lax.broadcasted_iota(jnp.int32, (ts, LANES), 1)
    first_half = (lane & half) == 0
    cos = jnp.where(first_half, trig, pltpu.roll(trig, half, 1))
    sin_signed = jnp.where(first_half, -pltpu.roll(trig, LANES - half, 1), trig)

    def rotary(t):
        swapped = jnp.where(first_half, pltpu.roll(t, LANES - half, 1), pltpu.roll(t, half, 1))
        return t * cos + swapped * sin_signed

    q = proj(0)
    k = proj(RET_WIDTH)
    k_rot = []
    for p in range(HEAD_PAIRS):
        ln = slice(p * LANES, (p + 1) * LANES)
        q_s[:, ln] = rotary(q[:, ln]).astype(BF16)
        k_rot.append(rotary(k[:, ln]) * (HEAD_DIM ** -0.5))
    k_t = jnp.concatenate(k_rot, axis=1).T
    dim_head0 = (lax.broadcasted_iota(jnp.int32, k_t.shape, 0) & HEAD_DIM) == 0
    kt_s[0] = k_t.astype(BF16)
    kt_s[1] = jnp.where(dim_head0, k_t, 0.0).astype(BF16)
    kt_s[2] = jnp.where(dim_head0, 0.0, k_t).astype(BF16)
    v = proj(2 * RET_WIDTH)
    lane_head0 = (lax.broadcasted_iota(jnp.int32, v.shape, 1) & HEAD_DIM) == 0
    v_s[0] = v.astype(BF16)
    v_s[1] = jnp.where(lane_head0, v, 0.0).astype(BF16)
    v_s[2] = jnp.where(lane_head0, 0.0, v).astype(BF16)

    for c in range(ts // CHUNK):
        rows = slice(c * CHUNK, (c + 1) * CHUNK)
        for p in range(HEAD_PAIRS):
            ln = slice(p * LANES, (p + 1) * LANES)
            qc = q_s[rows, ln]
            kcat = jnp.concatenate([kt_s[1, ln, rows], kt_s[2, ln, rows]], axis=1)
            scores = jnp.dot(qc, kcat, preferred_element_type=F32) * dmask_ref[p]
            vcat = jnp.concatenate([v_s[1, rows, ln], v_s[2, rows, ln]], axis=0)
            y_inner = jnp.dot(scores.astype(BF16), vcat, preferred_element_type=F32)
            r = r_s[p]
            y_cross = jnp.dot(qc, r.astype(BF16), preferred_element_type=F32) * xi_ref[p]
            y_s[rows, ln] = y_inner + y_cross
            vz = (v_s[0, rows, ln].astype(F32) * zeta_ref[p]).astype(BF16)
            kv = jnp.dot(kt_s[0, ln, rows], vz, preferred_element_type=F32)
            r_s[p] = r * rdecay_ref[p] + kv * rmask_ref[p]

    y_ret = _group_rms(y_s[...], retg_ref[...]) * _silu(proj(3 * RET_WIDTH))

    u_s[SUBLANES:ts + SUBLANES, :] = proj(4 * RET_WIDTH + CONV_WIDTH) * proj(4 * RET_WIDTH + 2 * CONV_WIDTH)
    y_conv = proj(4 * RET_WIDTH) * _causal_conv3(u_s, scw_ref, ts)
    y_conv = _group_rms(y_conv, cng_ref[...])

    cat = jnp.concatenate([y_ret.astype(BF16), y_conv.astype(BF16)], axis=1)
    mixed = _wdot(cat, wout_s[...])
    o_ref[...] = x + gate1 * mixed

    wup_bf_ref[...] = wup_ref[...].astype(BF16)
    wdown_bf_ref[...] = wdown_ref[...].astype(BF16)
    ca = _silu(c_ref[...])
    mod2_ref[...] += jnp.sum(ca * modw_ref[...], axis=0, keepdims=True)
    fmod_ref[...] += jnp.sum(ca * fmodw_ref[...], axis=0, keepdims=True)


def _retention_tables():
    h = np.arange(RET_HEADS, dtype=np.float64)
    log_gamma = np.log1p(-np.exp2(-5.0 - h))
    idx = np.arange(CHUNK, dtype=np.float64)
    rel = idx[:, None] - idx[None, :]
    dmask = np.where(rel[None] >= 0.0,
                     np.exp(np.maximum(rel, 0.0)[None] * log_gamma[:, None, None]), 0.0)
    dmask = dmask.reshape(HEAD_PAIRS, 2, CHUNK, CHUNK).transpose(0, 2, 1, 3)
    dmask = dmask.reshape(HEAD_PAIRS, CHUNK, 2 * CHUNK)
    zeta = np.exp((CHUNK - 1.0 - idx)[None, :] * log_gamma[:, None])
    xi = np.exp((idx + 1.0)[None, :] * log_gamma[:, None])
    chunk_decay = np.exp(CHUNK * log_gamma)

    def per_lane(t):
        t = t.reshape(HEAD_PAIRS, 2, CHUNK)
        return np.repeat(np.transpose(t, (0, 2, 1)), HEAD_DIM, axis=2)

    lane_head = np.arange(LANES) // HEAD_DIM
    rmask = np.broadcast_to(lane_head[:, None] == lane_head[None, :], (HEAD_PAIRS, LANES, LANES))
    rdecay = np.repeat(chunk_decay.reshape(HEAD_PAIRS, 1, 2), HEAD_DIM, axis=2)
    rdecay = np.broadcast_to(rdecay, (HEAD_PAIRS, LANES, LANES))
    return tuple(jnp.asarray(np.ascontiguousarray(t), dtype=F32)
                 for t in (dmask, per_lane(xi), per_lane(zeta), rdecay, rmask))


def _side_cast_spec(w, n_steps):
    k, n = w.shape
    nblk = next(b for b in range(n_steps, 0, -1)
                if k % b == 0 and (k // b) % (2 * SUBLANES) == 0)
    return pl.BlockSpec((k // nblk, n), lambda i: (jnp.minimum(i, nblk - 1), 0))


def _mix_call(x, pos_row, mod1, g1, w_in, retg, scw, cng, w_out, w_up, w_down,
              c_col, mod_w, mod_b, fmod_w, fmod_b):
    s = x.shape[0]
    ts = TS_MIX
    n_steps = s // ts
    half = HEAD_DIM // 2
    invf = (ROPE_BASE ** (-jnp.arange(half, dtype=F32) / half))[:, None]
    dmask, xi, zeta, rdecay, rmask = _retention_tables()
    row = lambda i: (i, 0)
    wup_spec = _side_cast_spec(w_up, n_steps)
    wdown_spec = _side_cast_spec(w_down, n_steps)
    n_mod2 = mod1.shape[1]
    assert mod_w.shape[1] == 2 * n_mod2
    n_fmod = fmod_w.shape[1]
    c_rows = c_col.shape[0] // n_steps
    return pl.pallas_call(
        _mix_kernel,
        out_shape=(jax.ShapeDtypeStruct((s, D_MODEL), F32),
                   jax.ShapeDtypeStruct(w_up.shape, BF16),
                   jax.ShapeDtypeStruct(w_down.shape, BF16),
                   jax.ShapeDtypeStruct((1, n_mod2), F32),
                   jax.ShapeDtypeStruct((1, n_fmod), F32)),
        grid=(n_steps,),
        in_specs=[
            pl.BlockSpec((ts, D_MODEL), row),
            pl.BlockSpec((1, ts), lambda i: (0, i)),
            _const_spec(mod1.shape),
            _const_spec(g1.shape),
            _const_spec(w_in.shape),
            _const_spec(invf.shape),
            _const_spec(dmask.shape),
            _const_spec(xi.shape),
            _const_spec(zeta.shape),
            _const_spec(rdecay.shape),
            _const_spec(rmask.shape),
            _const_spec(retg.shape),
            _const_spec(scw.shape),
            _const_spec(cng.shape),
            _const_spec(w_out.shape),
            wup_spec,
            wdown_spec,
            pl.BlockSpec((c_rows, 1), row),
            pl.BlockSpec((c_rows, n_mod2), lambda i: (i, 1)),
            pl.BlockSpec((1, n_mod2), lambda i: (0, 1)),
            pl.BlockSpec((c_rows, n_fmod), row),
            pl.BlockSpec((1, n_fmod), lambda i: (0, 0)),
        ],
        out_specs=(pl.BlockSpec((ts, D_MODEL), row), wup_spec, wdown_spec,
                   pl.BlockSpec((1, n_mod2), lambda i: (0, 0)),
                   pl.BlockSpec((1, n_fmod), lambda i: (0, 0))),
        scratch_shapes=[
            pltpu.VMEM((ts, RET_WIDTH), BF16),
            pltpu.VMEM((3, RET_WIDTH, ts), BF16),
            pltpu.VMEM((3, ts, RET_WIDTH), BF16),
            pltpu.VMEM((ts, RET_WIDTH), F32),
            pltpu.VMEM((ts + SUBLANES, CONV_WIDTH), F32),
            pltpu.VMEM((HEAD_PAIRS, LANES, LANES), F32),
            pltpu.VMEM(w_in.shape, BF16),
            pltpu.VMEM(w_out.shape, BF16),
        ],
        compiler_params=pltpu.CompilerParams(
            dimension_semantics=("arbitrary",), vmem_limit_bytes=VMEM_LIMIT_BYTES),
        name="token_mix",
    )(x, pos_row, mod1, g1, w_in, invf, dmask, xi, zeta, rdecay, rmask, retg, scw, cng, w_out,
      w_up, w_down, c_col, mod_w, mod_b, fmod_w, fmod_b)


def _ffn_kernel(x_ref, mod_ref, fmod_ref, g2_ref, wup_ref, fcw_ref, fcb_ref, wdown_ref, fg_ref,
                o_ref, a_s, act_s):
    ts = x_ref.shape[0]

    @pl.when(pl.program_id(0) == 0)
    def _():
        a_s[0:SUBLANES, :] = jnp.zeros((SUBLANES, D_FF), F32)

    shift2 = mod_ref[:, 0:D_MODEL]
    scale2 = mod_ref[:, D_MODEL:2 * D_MODEL]
    gate2 = mod_ref[:, 2 * D_MODEL:3 * D_MODEL]
    final_shift = fmod_ref[:, 0:D_MODEL]
    final_scale = fmod_ref[:, D_MODEL:2 * D_MODEL]
    x = x_ref[...]
    h = (_rms(x, g2_ref[...]) * (1.0 + scale2) + shift2).astype(BF16)
    for lo in range(0, D_FF, FFN_COL_CHUNK):
        cols = slice(lo, lo + FFN_COL_CHUNK)
        a_s[SUBLANES:ts + SUBLANES, cols] = _wdot(h, wup_ref[:, lo:lo + FFN_COL_CHUNK])
        val = _wdot(h, wup_ref[:, D_FF + lo:D_FF + lo + FFN_COL_CHUNK])
        w = fcw_ref[:, cols]
        a = (a_s[SUBLANES - 2:ts + SUBLANES - 2, cols] * w[0:1, :]
             + a_s[SUBLANES - 1:ts + SUBLANES - 1, cols] * w[1:2, :]
             + a_s[SUBLANES:ts + SUBLANES, cols] * w[2:3, :]) + fcb_ref[:, cols]
        act_s[:, cols] = (_silu(a) * val).astype(BF16)
    a_s[0:SUBLANES, :] = a_s[ts:ts + SUBLANES, :]
    sub = ts // FFN_DOWN_SPLIT
    for r in range(FFN_DOWN_SPLIT):
        rows = slice(r * sub, (r + 1) * sub)
        ffn = _wdot(act_s[rows, :], wdown_ref[...])
        x2 = x[rows] + gate2 * ffn
        o_ref[rows, :] = _rms(x2, fg_ref[...]) * (1.0 + final_scale) + final_shift


def _ffn_call(x, mod, fmod, g2, w_up, fcw, fcb, w_down, fg):
    s = x.shape[0]
    ts = TS_FFN
    row = lambda i: (i, 0)
    return pl.pallas_call(
        _ffn_kernel,
        out_shape=jax.ShapeDtypeStruct((s, D_MODEL), F32),
        grid=(s // ts,),
        in_specs=[
            pl.BlockSpec((ts, D_MODEL), row),
            _const_spec(mod.shape),
            _const_spec(fmod.shape),
            _const_spec(g2.shape),
            _const_spec(w_up.shape),
            _const_spec(fcw.shape),
            _const_spec(fcb.shape),
            _const_spec(w_down.shape),
            _const_spec(fg.shape),
        ],
        out_specs=pl.BlockSpec((ts, D_MODEL), row),
        scratch_shapes=[
            pltpu.VMEM((ts + SUBLANES, D_FF), F32),
            pltpu.VMEM((ts, D_FF), BF16),
        ],
        compiler_params=pltpu.CompilerParams(
            dimension_semantics=("arbitrary",), vmem_limit_bytes=VMEM_LIMIT_BYTES),
        name="convglu_ffn",
    )(x, mod, fmod, g2, w_up, fcw, fcb, w_down, fg)


def kernel(x, c, positions, mod_w, mod_b, norm1_g, w_in, ret_norm_g, short_conv_w, conv_norm_g,
           w_out, norm2_g, w_up, ffn_conv_w, ffn_conv_b, w_down, final_mod_w, final_mod_b,
           final_norm_g):
    b, s, d = x.shape
    assert (b, d) == (1, D_MODEL) and mod_w.shape[0] == 1 and s % max(TS_MIX, TS_FFN) == 0
    c_col = c.reshape(d, 1)
    mod1 = _mod_call(c_col, mod_w[0], mod_b, 3 * d)

    x2d = x.reshape(s, d)
    x1, w_up_bf, w_down_bf, mod2, fmod = _mix_call(
        x2d, positions, mod1, norm1_g, w_in[0], ret_norm_g, short_conv_w[0], conv_norm_g, w_out[0],
        w_up[0], w_down[0], c_col, mod_w[0], mod_b, final_mod_w, final_mod_b[None, :])
    out = _ffn_call(x1, mod2, fmod, norm2_g, w_up_bf, ffn_conv_w[0], ffn_conv_b, w_down_bf,
                    final_norm_g[None, :])
    return out.reshape(b, s, d)
```

```python
import numpy as np
import jax
import jax.numpy as jnp
from jax import lax
from jax.experimental import pallas as pl
from jax.experimental.pallas import tpu as pltpu

F32 = jnp.float32
BF16 = jnp.bfloat16

D_MODEL = 1024
RET_WIDTH = 512
RET_HEADS = 8
HEAD_DIM = 64
CONV_WIDTH = 512
GROUP_DIM = 64
CHUNK = 128
D_FF = 2816
ROPE_BASE = 10000.0
EPS = 1e-6
IN_COLS = 4 * RET_WIDTH + 3 * CONV_WIDTH

LANES = 128
SUBLANES = 8
HEAD_PAIRS = RET_WIDTH // LANES
VMEM_LIMIT_BYTES = 56 * 1024 * 1024

TS_MIX = 512
TS_FFN = 512
FFN_COL_CHUNK = 256
FFN_DOWN_BLOCKS = (256, 256)


def _const_spec(shape):
    return pl.BlockSpec(shape, lambda i: (0,) * len(shape), pipeline_mode=pl.Buffered(1))


def _rms_modulate(x, gain, scale, shift):
    ms = jnp.mean(x * x, axis=-1, keepdims=True)
    return x * lax.rsqrt(ms + EPS) * (gain * (1.0 + scale)) + shift


def _silu(x):
    return x * jax.nn.sigmoid(x)


def _wdot(act, w):
    return lax.dot_general(act, w, (((1,), (0,)), ((), ())), preferred_element_type=F32)


MOD_ROWS = 128


def _mod_kernel(c_ref, w_ref, b_ref, o_ref):
    @pl.when(pl.program_id(0) == 0)
    def _():
        o_ref[...] = b_ref[...]

    o_ref[...] += jnp.sum(_silu(c_ref[...]) * w_ref[...], axis=0, keepdims=True)


def _mod_call(c_col, w, b_row, n_cols):
    d = w.shape[0]
    return pl.pallas_call(
        _mod_kernel,
        out_shape=jax.ShapeDtypeStruct((1, n_cols), F32),
        grid=(d // MOD_ROWS,),
        in_specs=[
            pl.BlockSpec((MOD_ROWS, 1), lambda j: (j, 0)),
            pl.BlockSpec((MOD_ROWS, n_cols), lambda j: (j, 0)),
            pl.BlockSpec((1, n_cols), lambda j: (0, 0)),
        ],
        out_specs=pl.BlockSpec((1, n_cols), lambda j: (0, 0)),
        compiler_params=pltpu.CompilerParams(
            dimension_semantics=("arbitrary",), vmem_limit_bytes=VMEM_LIMIT_BYTES),
        name="adaln_mod",
    )(c_col, w, b_row)


def _group_rms(y, gain):
    lane = lax.broadcasted_iota(jnp.int32, (y.shape[0], LANES), 1)
    group0 = (lane & GROUP_DIM) == 0
    outs = []
    for lo in range(0, y.shape[1], LANES):
        yy = y[:, lo:lo + LANES]
        y2 = yy * yy
        s0 = jnp.sum(jnp.where(group0, y2, 0.0), axis=-1, keepdims=True)
        s1 = jnp.sum(jnp.where(group0, 0.0, y2), axis=-1, keepdims=True)
        ss = jnp.where(group0, s0, s1)
        outs.append(yy * lax.rsqrt(ss * (1.0 / GROUP_DIM) + EPS))
    return jnp.concatenate(outs, axis=1) * gain


def _causal_conv3(tail_ref, w_ref, ts):
    w = w_ref[...]
    y = (tail_ref[SUBLANES - 2:ts + SUBLANES - 2, :] * w[0:1, :]
         + tail_ref[SUBLANES - 1:ts + SUBLANES - 1, :] * w[1:2, :]
         + tail_ref[SUBLANES:ts + SUBLANES, :] * w[2:3, :])
    tail_ref[0:SUBLANES, :] = tail_ref[ts:ts + SUBLANES, :]
    return y


def _mix_kernel(x_ref, pos_ref, mod_ref, g1_ref, win_ref, invf_ref, dmask_ref, xi_ref, zeta_ref,
                rdecay_ref, rmask_ref, retg_ref, scw_ref, cng_ref, wout_ref, wup_ref, wdown_ref,
                c_ref, modw_ref, modb_ref, fmodw_ref, fmodb_ref,
                o_ref, wup_bf_ref, wdown_bf_ref, mod2_ref, fmod_ref, q_s, kt_s, v_s, y_s, u_s, r_s,
                win_s, wout_s):
    ts = x_ref.shape[0]

    @pl.when(pl.program_id(0) == 0)
    def _():
        r_s[...] = jnp.zeros_like(r_s)
        u_s[0:SUBLANES, :] = jnp.zeros((SUBLANES, CONV_WIDTH), F32)
        mod2_ref[...] = modb_ref[...]
        fmod_ref[...] = fmodb_ref[...]
        for col in range(0, IN_COLS, RET_WIDTH):
            win_s[:, col:col + RET_WIDTH] = win_ref[:, col:col + RET_WIDTH].astype(BF16)
        wout_s[...] = wout_ref[...].astype(BF16)

    x = x_ref[...]
    shift1 = mod_ref[:, 0:D_MODEL]
    scale1 = mod_ref[:, D_MODEL:2 * D_MODEL]
    gate1 = mod_ref[:, 2 * D_MODEL:3 * D_MODEL]
    h = _rms_modulate(x, g1_ref[...], scale1, shift1).astype(BF16)

    def proj(col):
        return _wdot(h, win_s[:, col:col + RET_WIDTH])

    half = HEAD_DIM // 2
    ang_t = invf_ref[...] * pos_ref[...].astype(F32)
    cos_t = jnp.cos(ang_t)
    sin_t = jnp.sin(ang_t)
    trig = jnp.concatenate([cos_t, sin_t, cos_t, sin_t], axis=0).T
    lane = lax.broadcasted_iota(jnp.int32, (ts, LANES), 1)
    first_half = (lane & half) == 0
    cos = jnp.where(first_half, trig, pltpu.roll(trig, half, 1))
    sin_signed = jnp.where(first_half, -pltpu.roll(trig, LANES - half, 1), trig)

    def rotary(t):
        swapped = jnp.where(first_half, pltpu.roll(t, LANES - half, 1), pltpu.roll(t, half, 1))
        return t * cos + swapped * sin_signed

    q = proj(0)
    k = proj(RET_WIDTH)
    k_rot = []
    for p in range(HEAD_PAIRS):
        ln = slice(p * LANES, (p + 1) * LANES)
        q_s[:, ln] = rotary(q[:, ln]).astype(BF16)
        k_rot.append(rotary(k[:, ln]) * (HEAD_DIM ** -0.5))
    k_t = jnp.concatenate(k_rot, axis=1).T
    dim_head0 = (lax.broadcasted_iota(jnp.int32, k_t.shape, 0) & HEAD_DIM) == 0
    kt_s[0] = k_t.astype(BF16)
    kt_s[1] = jnp.where(dim_head0, k_t, 0.0).astype(BF16)
    kt_s[2] = jnp.where(dim_head0, 0.0, k_t).astype(BF16)
    v = proj(2 * RET_WIDTH)
    lane_head0 = (lax.broadcasted_iota(jnp.int32, v.shape, 1) & HEAD_DIM) == 0
    v_s[0] = v.astype(BF16)
    v_s[1] = jnp.where(lane_head0, v, 0.0).astype(BF16)
    v_s[2] = jnp.where(lane_head0, 0.0, v).astype(BF16)

    for c in range(ts // CHUNK):
        rows = slice(c * CHUNK, (c + 1) * CHUNK)
        for p in range(HEAD_PAIRS):
            ln = slice(p * LANES, (p + 1) * LANES)
            qc = q_s[rows, ln]
            kcat = jnp.concatenate([kt_s[1, ln, rows], kt_s[2, ln, rows]], axis=1)
            scores = jnp.dot(qc, kcat, preferred_element_type=F32) * dmask_ref[p]
            vcat = jnp.concatenate([v_s[1, rows, ln], v_s[2, rows, ln]], axis=0)
            y_inner = jnp.dot(scores.astype(BF16), vcat, preferred_element_type=F32)
            r = r_s[p]
            y_cross = jnp.dot(qc, r.astype(BF16), preferred_element_type=F32) * xi_ref[p]
            y_s[rows, ln] = y_inner + y_cross
            vz = (v_s[0, rows, ln].astype(F32) * zeta_ref[p]).astype(BF16)
            kv = jnp.dot(kt_s[0, ln, rows], vz, preferred_element_type=F32)
            r_s[p] = r * rdecay_ref[p] + kv * rmask_ref[p]

    y_ret = _group_rms(y_s[...], retg_ref[...]) * _silu(proj(3 * RET_WIDTH))

    u_s[SUBLANES:ts + SUBLANES, :] = proj(4 * RET_WIDTH + CONV_WIDTH) * proj(4 * RET_WIDTH + 2 * CONV_WIDTH)
    y_conv = proj(4 * RET_WIDTH) * _causal_conv3(u_s, scw_ref, ts)
    y_conv = _group_rms(y_conv, cng_ref[...])

    cat = jnp.concatenate([y_ret.astype(BF16), y_conv.astype(BF16)], axis=1)
    mixed = _wdot(cat, wout_s[...])
    o_ref[...] = x + gate1 * mixed

    wup_bf_ref[...] = wup_ref[...].astype(BF16)
    wdown_bf_ref[...] = wdown_ref[...].astype(BF16)
    ca = _silu(c_ref[...])
    mod2_ref[...] += jnp.sum(ca * modw_ref[...], axis=0, keepdims=True)
    fmod_ref[...] += jnp.sum(ca * fmodw_ref[...], axis=0, keepdims=True)


def _retention_tables():
    h = np.arange(RET_HEADS, dtype=np.float64)
    log_gamma = np.log1p(-np.exp2(-5.0 - h))
    idx = np.arange(CHUNK, dtype=np.float64)
    rel = idx[:, None] - idx[None, :]
    dmask = np.where(rel[None] >= 0.0,
                     np.exp(np.maximum(rel, 0.0)[None] * log_gamma[:, None, None]), 0.0)
    dmask = dmask.reshape(HEAD_PAIRS, 2, CHUNK, CHUNK).transpose(0, 2, 1, 3)
    dmask = dmask.reshape(HEAD_PAIRS, CHUNK, 2 * CHUNK)
    zeta = np.exp((CHUNK - 1.0 - idx)[None, :] * log_gamma[:, None])
    xi = np.exp((idx + 1.0)[None, :] * log_gamma[:, None])
    chunk_decay = np.exp(CHUNK * log_gamma)

    def per_lane(t):
        t = t.reshape(HEAD_PAIRS, 2, CHUNK)
        return np.repeat(np.transpose(t, (0, 2, 1)), HEAD_DIM, axis=2)

    lane_head = np.arange(LANES) // HEAD_DIM
    rmask = np.broadcast_to(lane_head[:, None] == lane_head[None, :], (HEAD_PAIRS, LANES, LANES))
    rdecay = np.repeat(chunk_decay.reshape(HEAD_PAIRS, 1, 2), HEAD_DIM, axis=2)
    rdecay = np.broadcast_to(rdecay, (HEAD_PAIRS, LANES, LANES))
    return tuple(jnp.asarray(np.ascontiguousarray(t), dtype=F32)
                 for t in (dmask, per_lane(xi), per_lane(zeta), rdecay, rmask))


def _side_cast_spec(w, n_steps):
    k, n = w.shape
    nblk = next(b for b in range(n_steps, 0, -1)
                if k % b == 0 and (k // b) % (2 * SUBLANES) == 0)
    return pl.BlockSpec((k // nblk, n), lambda i: (jnp.minimum(i, nblk - 1), 0))


def _mix_call(x, pos_row, mod1, g1, w_in, retg, scw, cng, w_out, w_up, w_down,
              c_col, mod_w, mod_b, fmod_w, fmod_b):
    s = x.shape[0]
    ts = TS_MIX
    n_steps = s // ts
    half = HEAD_DIM // 2
    invf = (ROPE_BASE ** (-jnp.arange(half, dtype=F32) / half))[:, None]
    dmask, xi, zeta, rdecay, rmask = _retention_tables()
    row = lambda i: (i, 0)
    wup_spec = _side_cast_spec(w_up, n_steps)
    wdown_spec = _side_cast_spec(w_down, n_steps)
    n_mod2 = mod1.shape[1]
    assert mod_w.shape[1] == 2 * n_mod2
    n_fmod = fmod_w.shape[1]
    c_rows = c_col.shape[0] // n_steps
    return pl.pallas_call(
        _mix_kernel,
        out_shape=(jax.ShapeDtypeStruct((s, D_MODEL), F32),
                   jax.ShapeDtypeStruct(w_up.shape, BF16),
                   jax.ShapeDtypeStruct(w_down.shape, BF16),
                   jax.ShapeDtypeStruct((1, n_mod2), F32),
                   jax.ShapeDtypeStruct((1, n_fmod), F32)),
        grid=(n_steps,),
        in_specs=[
            pl.BlockSpec((ts, D_MODEL), row),
            pl.BlockSpec((1, ts), lambda i: (0, i)),
            _const_spec(mod1.shape),
            _const_spec(g1.shape),
            _const_spec(w_in.shape),
            _const_spec(invf.shape),
            _const_spec(dmask.shape),
            _const_spec(xi.shape),
            _const_spec(zeta.shape),
            _const_spec(rdecay.shape),
            _const_spec(rmask.shape),
            _const_spec(retg.shape),
            _const_spec(scw.shape),
            _const_spec(cng.shape),
            _const_spec(w_out.shape),
            wup_spec,
            wdown_spec,
            pl.BlockSpec((c_rows, 1), row),
            pl.BlockSpec((c_rows, n_mod2), lambda i: (i, 1)),
            pl.BlockSpec((1, n_mod2), lambda i: (0, 1)),
            pl.BlockSpec((c_rows, n_fmod), row),
            pl.BlockSpec((1, n_fmod), lambda i: (0, 0)),
        ],
        out_specs=(pl.BlockSpec((ts, D_MODEL), row), wup_spec, wdown_spec,
                   pl.BlockSpec((1, n_mod2), lambda i: (0, 0)),
                   pl.BlockSpec((1, n_fmod), lambda i: (0, 0))),
        scratch_shapes=[
            pltpu.VMEM((ts, RET_WIDTH), BF16),
            pltpu.VMEM((3, RET_WIDTH, ts), BF16),
            pltpu.VMEM((3, ts, RET_WIDTH), BF16),
            pltpu.VMEM((ts, RET_WIDTH), F32),
            pltpu.VMEM((ts + SUBLANES, CONV_WIDTH), F32),
            pltpu.VMEM((HEAD_PAIRS, LANES, LANES), F32),
            pltpu.VMEM(w_in.shape, BF16),
            pltpu.VMEM(w_out.shape, BF16),
        ],
        compiler_params=pltpu.CompilerParams(
            dimension_semantics=("arbitrary",), vmem_limit_bytes=VMEM_LIMIT_BYTES),
        name="token_mix",
    )(x, pos_row, mod1, g1, w_in, invf, dmask, xi, zeta, rdecay, rmask, retg, scw, cng, w_out,
      w_up, w_down, c_col, mod_w, mod_b, fmod_w, fmod_b)


def _ffn_kernel(x_ref, mod_ref, fmod_ref, g2_ref, wup_ref, fcw_ref, fcb_ref, wdown_ref, fg_ref,
                o_ref, a_s, act_s):
    ts = x_ref.shape[0]

    @pl.when(pl.program_id(0) == 0)
    def _():
        a_s[0:SUBLANES, :] = jnp.zeros((SUBLANES, D_FF), F32)

    shift2 = mod_ref[:, 0:D_MODEL]
    scale2 = mod_ref[:, D_MODEL:2 * D_MODEL]
    gate2 = mod_ref[:, 2 * D_MODEL:3 * D_MODEL]
    final_shift = fmod_ref[:, 0:D_MODEL]
    final_scale = fmod_ref[:, D_MODEL:2 * D_MODEL]
    x = x_ref[...]
    h = _rms_modulate(x, g2_ref[...], scale2, shift2).astype(BF16)
    for lo in range(0, D_FF, FFN_COL_CHUNK):
        cols = slice(lo, lo + FFN_COL_CHUNK)
        a_s[SUBLANES:ts + SUBLANES, cols] = _wdot(h, wup_ref[:, lo:lo + FFN_COL_CHUNK])
        val = _wdot(h, wup_ref[:, D_FF + lo:D_FF + lo + FFN_COL_CHUNK])
        w = fcw_ref[:, cols]
        a = (a_s[SUBLANES - 2:ts + SUBLANES - 2, cols] * w[0:1, :]
             + a_s[SUBLANES - 1:ts + SUBLANES - 1, cols] * w[1:2, :]
             + a_s[SUBLANES:ts + SUBLANES, cols] * w[2:3, :]) + fcb_ref[:, cols]
        act_s[:, cols] = (_silu(a) * val).astype(BF16)
    a_s[0:SUBLANES, :] = a_s[ts:ts + SUBLANES, :]
    r0 = 0
    for nrows in FFN_DOWN_BLOCKS:
        rows = slice(r0, r0 + nrows)
        r0 += nrows
        ffn = _wdot(act_s[rows, :], wdown_ref[...])
        x2 = x[rows] + gate2 * ffn
        o_ref[rows, :] = _rms_modulate(x2, fg_ref[...], final_scale, final_shift)


def _ffn_call(x, mod, fmod, g2, w_up, fcw, fcb, w_down, fg):
    s = x.shape[0]
    ts = TS_FFN
    row = lambda i: (i, 0)
    return pl.pallas_call(
        _ffn_kernel,
        out_shape=jax.ShapeDtypeStruct((s, D_MODEL), F32),
        grid=(s // ts,),
        in_specs=[
            pl.BlockSpec((ts, D_MODEL), row),
            _const_spec(mod.shape),
            _const_spec(fmod.shape),
            _const_spec(g2.shape),
            _const_spec(w_up.shape),
            _const_spec(fcw.shape),
            _const_spec(fcb.shape),
            _const_spec(w_down.shape),
            _const_spec(fg.shape),
        ],
        out_specs=pl.BlockSpec((ts, D_MODEL), row),
        scratch_shapes=[
            pltpu.VMEM((ts + SUBLANES, D_FF), F32),
            pltpu.VMEM((ts, D_FF), BF16),
        ],
        compiler_params=pltpu.CompilerParams(
            dimension_semantics=("arbitrary",), vmem_limit_bytes=VMEM_LIMIT_BYTES),
        name="convglu_ffn",
    )(x, mod, fmod, g2, w_up, fcw, fcb, w_down, fg)


def kernel(x, c, positions, mod_w, mod_b, norm1_g, w_in, ret_norm_g, short_conv_w, conv_norm_g,
           w_out, norm2_g, w_up, ffn_conv_w, ffn_conv_b, w_down, final_mod_w, final_mod_b,
           final_norm_g):
    b, s, d = x.shape
    assert (b, d) == (1, D_MODEL) and mod_w.shape[0] == 1 and s % max(TS_MIX, TS_FFN) == 0
    c_col = c.reshape(d, 1)
    mod1 = _mod_call(c_col, mod_w[0], mod_b, 3 * d)

    x2d = x.reshape(s, d)
    x1, w_up_bf, w_down_bf, mod2, fmod = _mix_call(
        x2d, positions, mod1, norm1_g, w_in[0], ret_norm_g, short_conv_w[0], conv_norm_g, w_out[0],
        w_up[0], w_down[0], c_col, mod_w[0], mod_b, final_mod_w, final_mod_b[None, :])
    out = _ffn_call(x1, mod2, fmod, norm2_g, w_up_bf, ffn_conv_w[0], ffn_conv_b, w_down_bf,
                    final_norm_g[None, :])
    return out.reshape(b, s, d)
```

```python
import numpy as np
import jax
import jax.numpy as jnp
from jax import lax
from jax.experimental import pallas as pl
from jax.experimental.pallas import tpu as pltpu

F32 = jnp.float32
BF16 = jnp.bfloat16

D_MODEL = 1024
RET_WIDTH = 512
RET_HEADS = 8
HEAD_DIM = 64
CONV_WIDTH = 512
GROUP_DIM = 64
CHUNK = 128
D_FF = 2816
ROPE_BASE = 10000.0
EPS = 1e-6
IN_COLS = 4 * RET_WIDTH + 3 * CONV_WIDTH

LANES = 128
SUBLANES = 8
HEAD_PAIRS = RET_WIDTH // LANES
VMEM_LIMIT_BYTES = 56 * 1024 * 1024

TS_MIX = 512
TS_FFN = 512
FFN_COL_CHUNK = 256
FFN_DOWN_BLOCKS = (256, 256)


def _const_spec(shape):
    return pl.BlockSpec(shape, lambda i: (0,) * len(shape), pipeline_mode=pl.Buffered(1))


def _rms_modulate(x, gain, scale, shift):
    ms = jnp.mean(x * x, axis=-1, keepdims=True)
    return x * lax.rsqrt(ms + EPS) * (gain * (1.0 + scale)) + shift


def _silu(x):
    return x * jax.nn.sigmoid(x)


def _wdot(act, w):
    return lax.dot_general(act, w, (((1,), (0,)), ((), ())), preferred_element_type=F32)


MOD_ROWS = 512


def _mod_kernel(c_ref, w_ref, b_ref, o_ref):
    @pl.when(pl.program_id(0) == 0)
    def _():
        o_ref[...] = b_ref[...]

    o_ref[...] += jnp.sum(_silu(c_ref[...]) * w_ref[...], axis=0, keepdims=True)


def _mod_call(c_col, w, b_row, n_cols):
    d = w.shape[0]
    return pl.pallas_call(
        _mod_kernel,
        out_shape=jax.ShapeDtypeStruct((1, n_cols), F32),
        grid=(d // MOD_ROWS,),
        in_specs=[
            pl.BlockSpec((MOD_ROWS, 1), lambda j: (j, 0)),
            pl.BlockSpec((MOD_ROWS, n_cols), lambda j: (j, 0)),
            pl.BlockSpec((1, n_cols), lambda j: (0, 0)),
        ],
        out_specs=pl.BlockSpec((1, n_cols), lambda j: (0, 0)),
        compiler_params=pltpu.CompilerParams(
            dimension_semantics=("arbitrary",), vmem_limit_bytes=VMEM_LIMIT_BYTES),
        name="adaln_mod",
    )(c_col, w, b_row)


def _group_rms(y, gain):
    lane = lax.broadcasted_iota(jnp.int32, (y.shape[0], LANES), 1)
    group0 = (lane & GROUP_DIM) == 0
    outs = []
    for lo in range(0, y.shape[1], LANES):
        yy = y[:, lo:lo + LANES]
        y2 = yy * yy
        s0 = jnp.sum(jnp.where(group0, y2, 0.0), axis=-1, keepdims=True)
        s1 = jnp.sum(jnp.where(group0, 0.0, y2), axis=-1, keepdims=True)
        ss = jnp.where(group0, s0, s1)
        outs.append(yy * lax.rsqrt(ss * (1.0 / GROUP_DIM) + EPS))
    return jnp.concatenate(outs, axis=1) * gain


def _causal_conv3(tail_ref, w_ref, ts):
    w = w_ref[...]
    y = (tail_ref[SUBLANES - 2:ts + SUBLANES - 2, :] * w[0:1, :]
         + tail_ref[SUBLANES - 1:ts + SUBLANES - 1, :] * w[1:2, :]
         + tail_ref[SUBLANES:ts + SUBLANES, :] * w[2:3, :])
    tail_ref[0:SUBLANES, :] = tail_ref[ts:ts + SUBLANES, :]
    return y


def _mix_kernel(x_ref, pos_ref, mod_ref, g1_ref, win_ref, invf_ref, dmask_ref, xi_ref, zeta_ref,
                rdecay_ref, rmask_ref, retg_ref, scw_ref, cng_ref, wout_ref, wup_ref, wdown_ref,
                c_ref, modw_ref, modb_ref, fmodw_ref, fmodb_ref,
                o_ref, wup_bf_ref, wdown_bf_ref, mod2_ref, fmod_ref, q_s, kt_s, v_s, y_s, u_s, r_s,
                win_s, wout_s):
    ts = x_ref.shape[0]

    @pl.when(pl.program_id(0) == 0)
    def _():
        r_s[...] = jnp.zeros_like(r_s)
        u_s[0:SUBLANES, :] = jnp.zeros((SUBLANES, CONV_WIDTH), F32)
        mod2_ref[...] = modb_ref[...]
        fmod_ref[...] = fmodb_ref[...]
        for col in range(0, IN_COLS, RET_WIDTH):
            win_s[:, col:col + RET_WIDTH] = win_ref[:, col:col + RET_WIDTH].astype(BF16)
        wout_s[...] = wout_ref[...].astype(BF16)

    x = x_ref[...]
    shift1 = mod_ref[:, 0:D_MODEL]
    scale1 = mod_ref[:, D_MODEL:2 * D_MODEL]
    gate1 = mod_ref[:, 2 * D_MODEL:3 * D_MODEL]
    h = _rms_modulate(x, g1_ref[...], scale1, shift1).astype(BF16)

    def proj(col):
        return _wdot(h, win_s[:, col:col + RET_WIDTH])

    half = HEAD_DIM // 2
    ang_t = invf_ref[...] * pos_ref[...].astype(F32)
    cos_t = jnp.cos(ang_t)
    sin_t = jnp.sin(ang_t)
    trig = jnp.concatenate([cos_t, sin_t, cos_t, sin_t], axis=0).T
    lane = lax.broadcasted_iota(jnp.int32, (ts, LANES), 1)
    first_half = (lane & half) == 0
    cos = jnp.where(first_half, trig, pltpu.roll(trig, half, 1))
    sin_signed = jnp.where(first_half, -pltpu.roll(trig, LANES - half, 1), trig)

    def rotary(t):
        swapped = jnp.where(first_half, pltpu.roll(t, LANES - half, 1), pltpu.roll(t, half, 1))
        return t * cos + swapped * sin_signed

    q = proj(0)
    k = proj(RET_WIDTH)
    k_rot = []
    for p in range(HEAD_PAIRS):
        ln = slice(p * LANES, (p + 1) * LANES)
        q_s[:, ln] = rotary(q[:, ln]).astype(BF16)
        k_rot.append(rotary(k[:, ln]) * (HEAD_DIM ** -0.5))
    k_t = jnp.concatenate(k_rot, axis=1).T
    dim_head0 = (lax.broadcasted_iota(jnp.int32, k_t.shape, 0) & HEAD_DIM) == 0
    kt_s[0] = k_t.astype(BF16)
    kt_s[1] = jnp.where(dim_head0, k_t, 0.0).astype(BF16)
    kt_s[2] = jnp.where(dim_head0, 0.0, k_t).astype(BF16)
    v = proj(2 * RET_WIDTH)
    lane_head0 = (lax.broadcasted_iota(jnp.int32, v.shape, 1) & HEAD_DIM) == 0
    v_s[0] = v.astype(BF16)
    v_s[1] = jnp.where(lane_head0, v, 0.0).astype(BF16)
    v_s[2] = jnp.where(lane_head0, 0.0, v).astype(BF16)

    for c in range(ts // CHUNK):
        rows = slice(c * CHUNK, (c + 1) * CHUNK)
        for p in range(HEAD_PAIRS):
            ln = slice(p * LANES, (p + 1) * LANES)
            qc = q_s[rows, ln]
            kcat = jnp.concatenate([kt_s[1, ln, rows], kt_s[2, ln, rows]], axis=1)
            scores = jnp.dot(qc, kcat, preferred_element_type=F32) * dmask_ref[p]
            vcat = jnp.concatenate([v_s[1, rows, ln], v_s[2, rows, ln]], axis=0)
            y_inner = jnp.dot(scores.astype(BF16), vcat, preferred_element_type=F32)
            r = r_s[p]
            y_cross = jnp.dot(qc, r.astype(BF16), preferred_element_type=F32) * xi_ref[p]
            y_s[rows, ln] = y_inner + y_cross
            vz = (v_s[0, rows, ln].astype(F32) * zeta_ref[p]).astype(BF16)
            kv = jnp.dot(kt_s[0, ln, rows], vz, preferred_element_type=F32)
            r_s[p] = r * rdecay_ref[p] + kv * rmask_ref[p]

    y_ret = _group_rms(y_s[...], retg_ref[...]) * _silu(proj(3 * RET_WIDTH))

    u_s[SUBLANES:ts + SUBLANES, :] = proj(4 * RET_WIDTH + CONV_WIDTH) * proj(4 * RET_WIDTH + 2 * CONV_WIDTH)
    y_conv = proj(4 * RET_WIDTH) * _causal_conv3(u_s, scw_ref, ts)
    y_conv = _group_rms(y_conv, cng_ref[...])

    cat = jnp.concatenate([y_ret.astype(BF16), y_conv.astype(BF16)], axis=1)
    mixed = _wdot(cat, wout_s[...])
    o_ref[...] = x + gate1 * mixed

    wup_bf_ref[...] = wup_ref[...].astype(BF16)
    wdown_bf_ref[...] = wdown_ref[...].astype(BF16)
    ca = _silu(c_ref[...])
    mod2_ref[...] += jnp.sum(ca * modw_ref[...], axis=0, keepdims=True)
    fmod_ref[...] += jnp.sum(ca * fmodw_ref[...], axis=0, keepdims=True)


def _retention_tables():
    h = np.arange(RET_HEADS, dtype=np.float64)
    log_gamma = np.log1p(-np.exp2(-5.0 - h))
    idx = np.arange(CHUNK, dtype=np.float64)
    rel = idx[:, None] - idx[None, :]
    dmask = np.where(rel[None] >= 0.0,
                     np.exp(np.maximum(rel, 0.0)[None] * log_gamma[:, None, None]), 0.0)
    dmask = dmask.reshape(HEAD_PAIRS, 2, CHUNK, CHUNK).transpose(0, 2, 1, 3)
    dmask = dmask.reshape(HEAD_PAIRS, CHUNK, 2 * CHUNK)
    zeta = np.exp((CHUNK - 1.0 - idx)[None, :] * log_gamma[:, None])
    xi = np.exp((idx + 1.0)[None, :] * log_gamma[:, None])
    chunk_decay = np.exp(CHUNK * log_gamma)

    def per_lane(t):
        t = t.reshape(HEAD_PAIRS, 2, CHUNK)
        return np.repeat(np.transpose(t, (0, 2, 1)), HEAD_DIM, axis=2)

    lane_head = np.arange(LANES) // HEAD_DIM
    rmask = np.broadcast_to(lane_head[:, None] == lane_head[None, :], (HEAD_PAIRS, LANES, LANES))
    rdecay = np.repeat(chunk_decay.reshape(HEAD_PAIRS, 1, 2), HEAD_DIM, axis=2)
    rdecay = np.broadcast_to(rdecay, (HEAD_PAIRS, LANES, LANES))
    return tuple(jnp.asarray(np.ascontiguousarray(t), dtype=F32)
                 for t in (dmask, per_lane(xi), per_lane(zeta), rdecay, rmask))


def _side_cast_spec(w, n_steps):
    k, n = w.shape
    nblk = next(b for b in range(n_steps, 0, -1)
                if k % b == 0 and (k // b) % (2 * SUBLANES) == 0)
    return pl.BlockSpec((k // nblk, n), lambda i: (jnp.minimum(i, nblk - 1), 0))


def _mix_call(x, pos_row, mod1, g1, w_in, retg, scw, cng, w_out, w_up, w_down,
              c_col, mod_w, mod_b, fmod_w, fmod_b):
    s = x.shape[0]
    ts = TS_MIX
    n_steps = s // ts
    half = HEAD_DIM // 2
    invf = (ROPE_BASE ** (-jnp.arange(half, dtype=F32) / half))[:, None]
    dmask, xi, zeta, rdecay, rmask = _retention_tables()
    row = lambda i: (i, 0)
    wup_spec = _side_cast_spec(w_up, n_steps)
    wdown_spec = _side_cast_spec(w_down, n_steps)
    n_mod2 = mod1.shape[1]
    assert mod_w.shape[1] == 2 * n_mod2
    n_fmod = fmod_w.shape[1]
    c_rows = c_col.shape[0] // n_steps
    return pl.pallas_call(
        _mix_kernel,
        out_shape=(jax.ShapeDtypeStruct((s, D_MODEL), F32),
                   jax.ShapeDtypeStruct(w_up.shape, BF16),
                   jax.ShapeDtypeStruct(w_down.shape, BF16),
                   jax.ShapeDtypeStruct((1, n_mod2), F32),
                   jax.ShapeDtypeStruct((1, n_fmod), F32)),
        grid=(n_steps,),
        in_specs=[
            pl.BlockSpec((ts, D_MODEL), row),
            pl.BlockSpec((1, ts), lambda i: (0, i)),
            _const_spec(mod1.shape),
            _const_spec(g1.shape),
            _const_spec(w_in.shape),
            _const_spec(invf.shape),
            _const_spec(dmask.shape),
            _const_spec(xi.shape),
            _const_spec(zeta.shape),
            _const_spec(rdecay.shape),
            _const_spec(rmask.shape),
            _const_spec(retg.shape),
            _const_spec(scw.shape),
            _const_spec(cng.shape),
            _const_spec(w_out.shape),
            wup_spec,
            wdown_spec,
            pl.BlockSpec((c_rows, 1), row),
            pl.BlockSpec((c_rows, n_mod2), lambda i: (i, 1)),
            pl.BlockSpec((1, n_mod2), lambda i: (0, 1)),
            pl.BlockSpec((c_rows, n_fmod), row),
            pl.BlockSpec((1, n_fmod), lambda i: (0, 0)),
        ],
        out_specs=(pl.BlockSpec((ts, D_MODEL), row), wup_spec, wdown_spec,
                   pl.BlockSpec((1, n_mod2), lambda i: (0, 0)),
                   pl.BlockSpec((1, n_fmod), lambda i: (0, 0))),
        scratch_shapes=[
            pltpu.VMEM((ts, RET_WIDTH), BF16),
            pltpu.VMEM((3, RET_WIDTH, ts), BF16),
            pltpu.VMEM((3, ts, RET_WIDTH), BF16),
            pltpu.VMEM((ts, RET_WIDTH), F32),
            pltpu.VMEM((ts + SUBLANES, CONV_WIDTH), F32),
            pltpu.VMEM((HEAD_PAIRS, LANES, LANES), F32),
            pltpu.VMEM(w_in.shape, BF16),
            pltpu.VMEM(w_out.shape, BF16),
        ],
        compiler_params=pltpu.CompilerParams(
            dimension_semantics=("arbitrary",), vmem_limit_bytes=VMEM_LIMIT_BYTES),
        name="token_mix",
    )(x, pos_row, mod1, g1, w_in, invf, dmask, xi, zeta, rdecay, rmask, retg, scw, cng, w_out,
      w_up, w_down, c_col, mod_w, mod_b, fmod_w, fmod_b)


def _ffn_kernel(x_ref, mod_ref, fmod_ref, g2_ref, wup_ref, fcw_ref, fcb_ref, wdown_ref, fg_ref,
                o_ref, a1_s, a2_s, act_s):
    ts = x_ref.shape[0]
    head = slice(SUBLANES, 2 * SUBLANES)
    past = slice(ts + SUBLANES, ts + 2 * SUBLANES)

    @pl.when(pl.program_id(0) == 0)
    def _():
        a1_s[past, :] = jnp.zeros((SUBLANES, D_FF), F32)
        a2_s[past, :] = jnp.zeros((SUBLANES, D_FF), F32)

    a1_s[head, :] = a1_s[past, :]
    a2_s[head, :] = a2_s[past, :]

    shift2 = mod_ref[:, 0:D_MODEL]
    scale2 = mod_ref[:, D_MODEL:2 * D_MODEL]
    gate2 = mod_ref[:, 2 * D_MODEL:3 * D_MODEL]
    final_shift = fmod_ref[:, 0:D_MODEL]
    final_scale = fmod_ref[:, D_MODEL:2 * D_MODEL]
    x = x_ref[...]
    h = _rms_modulate(x, g2_ref[...], scale2, shift2).astype(BF16)
    for lo in range(0, D_FF, FFN_COL_CHUNK):
        cols = slice(lo, lo + FFN_COL_CHUNK)
        a0 = _wdot(h, wup_ref[:, lo:lo + FFN_COL_CHUNK])
        val = _wdot(h, wup_ref[:, D_FF + lo:D_FF + lo + FFN_COL_CHUNK])
        a1_s[SUBLANES + 1:ts + SUBLANES + 1, cols] = a0
        a2_s[SUBLANES + 2:ts + SUBLANES + 2, cols] = a0
        w = fcw_ref[:, cols]
        a = (a2_s[SUBLANES:ts + SUBLANES, cols] * w[0:1, :]
             + a1_s[SUBLANES:ts + SUBLANES, cols] * w[1:2, :]
             + a0 * w[2:3, :]) + fcb_ref[:, cols]
        act_s[:, cols] = (_silu(a) * val).astype(BF16)
    r0 = 0
    for nrows in FFN_DOWN_BLOCKS:
        rows = slice(r0, r0 + nrows)
        r0 += nrows
        ffn = _wdot(act_s[rows, :], wdown_ref[...])
        x2 = x[rows] + gate2 * ffn
        o_ref[rows, :] = _rms_modulate(x2, fg_ref[...], final_scale, final_shift)


def _ffn_call(x, mod, fmod, g2, w_up, fcw, fcb, w_down, fg):
    s = x.shape[0]
    ts = TS_FFN
    row = lambda i: (i, 0)
    return pl.pallas_call(
        _ffn_kernel,
        out_shape=jax.ShapeDtypeStruct((s, D_MODEL), F32),
        grid=(s // ts,),
        in_specs=[
            pl.BlockSpec((ts, D_MODEL), row),
            _const_spec(mod.shape),
            _const_spec(fmod.shape),
            _const_spec(g2.shape),
            _const_spec(w_up.shape),
            _const_spec(fcw.shape),
            _const_spec(fcb.shape),
            _const_spec(w_down.shape),
            _const_spec(fg.shape),
        ],
        out_specs=pl.BlockSpec((ts, D_MODEL), row),
        scratch_shapes=[
            pltpu.VMEM((ts + 2 * SUBLANES, D_FF), F32),
            pltpu.VMEM((ts + 2 * SUBLANES, D_FF), F32),
            pltpu.VMEM((ts, D_FF), BF16),
        ],
        compiler_params=pltpu.CompilerParams(
            dimension_semantics=("arbitrary",), vmem_limit_bytes=VMEM_LIMIT_BYTES),
        name="convglu_ffn",
    )(x, mod, fmod, g2, w_up, fcw, fcb, w_down, fg)


def kernel(x, c, positions, mod_w, mod_b, norm1_g, w_in, ret_norm_g, short_conv_w, conv_norm_g,
           w_out, norm2_g, w_up, ffn_conv_w, ffn_conv_b, w_down, final_mod_w, final_mod_b,
           final_norm_g):
    b, s, d = x.shape
    assert (b, d) == (1, D_MODEL) and mod_w.shape[0] == 1 and s % max(TS_MIX, TS_FFN) == 0
    c_col = c.reshape(d, 1)
    mod1 = _mod_call(c_col, mod_w[0], mod_b, 3 * d)

    x2d = x.reshape(s, d)
    x1, w_up_bf, w_down_bf, mod2, fmod = _mix_call(
        x2d, positions, mod1, norm1_g, w_in[0], ret_norm_g, short_conv_w[0], conv_norm_g, w_out[0],
        w_up[0], w_down[0], c_col, mod_w[0], mod_b, final_mod_w, final_mod_b[None, :])
    out = _ffn_call(x1, mod2, fmod, norm2_g, w_up_bf, ffn_conv_w[0], ffn_conv_b, w_down_bf,
                    final_norm_g[None, :])
    return out.reshape(b, s, d)
```

```python
import numpy as np
import jax
import jax.numpy as jnp
from jax import lax
from jax.experimental import pallas as pl
from jax.experimental.pallas import tpu as pltpu

F32 = jnp.float32
BF16 = jnp.bfloat16

D_MODEL = 1024
RET_WIDTH = 512
RET_HEADS = 8
HEAD_DIM = 64
CONV_WIDTH = 512
GROUP_DIM = 64
CHUNK = 128
D_FF = 2816
ROPE_BASE = 10000.0
EPS = 1e-6
IN_COLS = 4 * RET_WIDTH + 3 * CONV_WIDTH

LANES = 128
SUBLANES = 8
HEAD_PAIRS = RET_WIDTH // LANES
VMEM_LIMIT_BYTES = 56 * 1024 * 1024

TS_MIX = 512
TS_FFN = 512
FFN_COL_CHUNK = 256
FFN_ROW_BLOCKS = (256, 256)


def _const_spec(shape):
    return pl.BlockSpec(shape, lambda i: (0,) * len(shape), pipeline_mode=pl.Buffered(1))


def _rms_modulate(x, gain, scale, shift):
    ms = jnp.mean(x * x, axis=-1, keepdims=True)
    return x * lax.rsqrt(ms + EPS) * (gain * (1.0 + scale)) + shift


def _silu(x):
    return x * jax.nn.sigmoid(x)


def _wdot(act, w):
    return lax.dot_general(act, w, (((1,), (0,)), ((), ())), preferred_element_type=F32)


MOD_ROWS = 512


def _mod_kernel(c_ref, w_ref, b_ref, o_ref):
    @pl.when(pl.program_id(0) == 0)
    def _():
        o_ref[...] = b_ref[...]

    o_ref[...] += jnp.sum(_silu(c_ref[...]) * w_ref[...], axis=0, keepdims=True)


def _mod_call(c_col, w, b_row, n_cols):
    d = w.shape[0]
    return pl.pallas_call(
        _mod_kernel,
        out_shape=jax.ShapeDtypeStruct((1, n_cols), F32),
        grid=(d // MOD_ROWS,),
        in_specs=[
            pl.BlockSpec((MOD_ROWS, 1), lambda j: (j, 0)),
            pl.BlockSpec((MOD_ROWS, n_cols), lambda j: (j, 0)),
            pl.BlockSpec((1, n_cols), lambda j: (0, 0)),
        ],
        out_specs=pl.BlockSpec((1, n_cols), lambda j: (0, 0)),
        compiler_params=pltpu.CompilerParams(
            dimension_semantics=("arbitrary",), vmem_limit_bytes=VMEM_LIMIT_BYTES),
        name="adaln_mod",
    )(c_col, w, b_row)


def _group_rms(y, gain):
    lane = lax.broadcasted_iota(jnp.int32, (y.shape[0], LANES), 1)
    group0 = (lane & GROUP_DIM) == 0
    outs = []
    for lo in range(0, y.shape[1], LANES):
        yy = y[:, lo:lo + LANES]
        y2 = yy * yy
        s0 = jnp.sum(jnp.where(group0, y2, 0.0), axis=-1, keepdims=True)
        s1 = jnp.sum(jnp.where(group0, 0.0, y2), axis=-1, keepdims=True)
        ss = jnp.where(group0, s0, s1)
        outs.append(yy * lax.rsqrt(ss * (1.0 / GROUP_DIM) + EPS))
    return jnp.concatenate(outs, axis=1) * gain


def _delay_init(d1_ref, d2_ref, ts, first_step):
    head = slice(SUBLANES, 2 * SUBLANES)
    past = slice(ts + SUBLANES, ts + 2 * SUBLANES)

    @pl.when(first_step)
    def _():
        d1_ref[past, :] = jnp.zeros((SUBLANES, d1_ref.shape[1]), F32)
        d2_ref[past, :] = jnp.zeros((SUBLANES, d2_ref.shape[1]), F32)

    d1_ref[head, :] = d1_ref[past, :]
    d2_ref[head, :] = d2_ref[past, :]


def _causal_conv3(d1_ref, d2_ref, u, w, row0, cols):
    n = u.shape[0]
    lo = SUBLANES + row0
    d1_ref[lo + 1:lo + n + 1, cols] = u
    d2_ref[lo + 2:lo + n + 2, cols] = u
    return (d2_ref[lo:lo + n, cols] * w[0:1, :] + d1_ref[lo:lo + n, cols] * w[1:2, :]
            + u * w[2:3, :])


def _mix_kernel(x_ref, pos_ref, mod_ref, g1_ref, win_ref, invf_ref, dmask_ref, xi_ref, zeta_ref,
                rdecay_ref, rmask_ref, retg_ref, scw_ref, cng_ref, wout_ref, wup_ref, wdown_ref,
                c_ref, modw_ref, modb_ref, fmodw_ref, fmodb_ref,
                o_ref, wup_bf_ref, wdown_bf_ref, mod2_ref, fmod_ref, q_s, kt_s, v_s, y_s, u1_s, u2_s,
                r_s, win_s, wout_s):
    ts = x_ref.shape[0]
    _delay_init(u1_s, u2_s, ts, pl.program_id(0) == 0)

    @pl.when(pl.program_id(0) == 0)
    def _():
        r_s[...] = jnp.zeros_like(r_s)
        mod2_ref[...] = modb_ref[...]
        fmod_ref[...] = fmodb_ref[...]
        for col in range(0, IN_COLS, RET_WIDTH):
            win_s[:, col:col + RET_WIDTH] = win_ref[:, col:col + RET_WIDTH].astype(BF16)
        wout_s[...] = wout_ref[...].astype(BF16)

    x = x_ref[...]
    shift1 = mod_ref[:, 0:D_MODEL]
    scale1 = mod_ref[:, D_MODEL:2 * D_MODEL]
    gate1 = mod_ref[:, 2 * D_MODEL:3 * D_MODEL]
    h = _rms_modulate(x, g1_ref[...], scale1, shift1).astype(BF16)

    def proj(col):
        return _wdot(h, win_s[:, col:col + RET_WIDTH])

    half = HEAD_DIM // 2
    ang_t = invf_ref[...] * pos_ref[...].astype(F32)
    cos_t = jnp.cos(ang_t)
    sin_t = jnp.sin(ang_t)
    trig = jnp.concatenate([cos_t, sin_t, cos_t, sin_t], axis=0).T
    lane = lax.broadcasted_iota(jnp.int32, (ts, LANES), 1)
    first_half = (lane & half) == 0
    cos = jnp.where(first_half, trig, pltpu.roll(trig, half, 1))
    sin_signed = jnp.where(first_half, -pltpu.roll(trig, LANES - half, 1), trig)

    def rotary(t):
        swapped = jnp.where(first_half, pltpu.roll(t, LANES - half, 1), pltpu.roll(t, half, 1))
        return t * cos + swapped * sin_signed

    q = proj(0)
    k = proj(RET_WIDTH)
    k_rot = []
    for p in range(HEAD_PAIRS):
        ln = slice(p * LANES, (p + 1) * LANES)
        q_s[:, ln] = rotary(q[:, ln]).astype(BF16)
        k_rot.append(rotary(k[:, ln]) * (HEAD_DIM ** -0.5))
    k_t = jnp.concatenate(k_rot, axis=1).T
    dim_head0 = (lax.broadcasted_iota(jnp.int32, k_t.shape, 0) & HEAD_DIM) == 0
    kt_s[0] = k_t.astype(BF16)
    kt_s[1] = jnp.where(dim_head0, k_t, 0.0).astype(BF16)
    kt_s[2] = jnp.where(dim_head0, 0.0, k_t).astype(BF16)
    v = proj(2 * RET_WIDTH)
    lane_head0 = (lax.broadcasted_iota(jnp.int32, v.shape, 1) & HEAD_DIM) == 0
    v_s[0] = v.astype(BF16)
    v_s[1] = jnp.where(lane_head0, v, 0.0).astype(BF16)
    v_s[2] = jnp.where(lane_head0, 0.0, v).astype(BF16)

    for c in range(ts // CHUNK):
        rows = slice(c * CHUNK, (c + 1) * CHUNK)
        for p in range(HEAD_PAIRS):
            ln = slice(p * LANES, (p + 1) * LANES)
            qc = q_s[rows, ln]
            kcat = jnp.concatenate([kt_s[1, ln, rows], kt_s[2, ln, rows]], axis=1)
            scores = jnp.dot(qc, kcat, preferred_element_type=F32) * dmask_ref[p]
            vcat = jnp.concatenate([v_s[1, rows, ln], v_s[2, rows, ln]], axis=0)
            y_inner = jnp.dot(scores.astype(BF16), vcat, preferred_element_type=F32)
            r = r_s[p]
            y_cross = jnp.dot(qc, r.astype(BF16), preferred_element_type=F32) * xi_ref[p]
            y_s[rows, ln] = y_inner + y_cross
            vz = (v_s[0, rows, ln].astype(F32) * zeta_ref[p]).astype(BF16)
            kv = jnp.dot(kt_s[0, ln, rows], vz, preferred_element_type=F32)
            r_s[p] = r * rdecay_ref[p] + kv * rmask_ref[p]

    y_ret = _group_rms(y_s[...], retg_ref[...]) * _silu(proj(3 * RET_WIDTH))

    u = proj(4 * RET_WIDTH + CONV_WIDTH) * proj(4 * RET_WIDTH + 2 * CONV_WIDTH)
    y_conv = proj(4 * RET_WIDTH) * _causal_conv3(u1_s, u2_s, u, scw_ref[...], 0, slice(None))
    y_conv = _group_rms(y_conv, cng_ref[...])

    cat = jnp.concatenate([y_ret.astype(BF16), y_conv.astype(BF16)], axis=1)
    mixed = _wdot(cat, wout_s[...])
    o_ref[...] = x + gate1 * mixed

    wup_bf_ref[...] = wup_ref[...].astype(BF16)
    wdown_bf_ref[...] = wdown_ref[...].astype(BF16)
    ca = _silu(c_ref[...])
    mod2_ref[...] += jnp.sum(ca * modw_ref[...], axis=0, keepdims=True)
    fmod_ref[...] += jnp.sum(ca * fmodw_ref[...], axis=0, keepdims=True)


def _retention_tables():
    h = np.arange(RET_HEADS, dtype=np.float64)
    log_gamma = np.log1p(-np.exp2(-5.0 - h))
    idx = np.arange(CHUNK, dtype=np.float64)
    rel = idx[:, None] - idx[None, :]
    dmask = np.where(rel[None] >= 0.0,
                     np.exp(np.maximum(rel, 0.0)[None] * log_gamma[:, None, None]), 0.0)
    dmask = dmask.reshape(HEAD_PAIRS, 2, CHUNK, CHUNK).transpose(0, 2, 1, 3)
    dmask = dmask.reshape(HEAD_PAIRS, CHUNK, 2 * CHUNK)
    zeta = np.exp((CHUNK - 1.0 - idx)[None, :] * log_gamma[:, None])
    xi = np.exp((idx + 1.0)[None, :] * log_gamma[:, None])
    chunk_decay = np.exp(CHUNK * log_gamma)

    def per_lane(t):
        t = t.reshape(HEAD_PAIRS, 2, CHUNK)
        return np.repeat(np.transpose(t, (0, 2, 1)), HEAD_DIM, axis=2)

    lane_head = np.arange(LANES) // HEAD_DIM
    rmask = np.broadcast_to(lane_head[:, None] == lane_head[None, :], (HEAD_PAIRS, LANES, LANES))
    rdecay = np.repeat(chunk_decay.reshape(HEAD_PAIRS, 1, 2), HEAD_DIM, axis=2)
    rdecay = np.broadcast_to(rdecay, (HEAD_PAIRS, LANES, LANES))
    return tuple(jnp.asarray(np.ascontiguousarray(t), dtype=F32)
                 for t in (dmask, per_lane(xi), per_lane(zeta), rdecay, rmask))


def _side_cast_spec(w, n_steps):
    k, n = w.shape
    nblk = next(b for b in range(n_steps, 0, -1)
                if k % b == 0 and (k // b) % (2 * SUBLANES) == 0)
    return pl.BlockSpec((k // nblk, n), lambda i: (jnp.minimum(i, nblk - 1), 0))


def _mix_call(x, pos_row, mod1, g1, w_in, retg, scw, cng, w_out, w_up, w_down,
              c_col, mod_w, mod_b, fmod_w, fmod_b):
    s = x.shape[0]
    ts = TS_MIX
    n_steps = s // ts
    half = HEAD_DIM // 2
    invf = (ROPE_BASE ** (-jnp.arange(half, dtype=F32) / half))[:, None]
    dmask, xi, zeta, rdecay, rmask = _retention_tables()
    row = lambda i: (i, 0)
    wup_spec = _side_cast_spec(w_up, n_steps)
    wdown_spec = _side_cast_spec(w_down, n_steps)
    n_mod2 = mod1.shape[1]
    assert mod_w.shape[1] == 2 * n_mod2
    n_fmod = fmod_w.shape[1]
    c_rows = c_col.shape[0] // n_steps
    return pl.pallas_call(
        _mix_kernel,
        out_shape=(jax.ShapeDtypeStruct((s, D_MODEL), F32),
                   jax.ShapeDtypeStruct(w_up.shape, BF16),
                   jax.ShapeDtypeStruct(w_down.shape, BF16),
                   jax.ShapeDtypeStruct((1, n_mod2), F32),
                   jax.ShapeDtypeStruct((1, n_fmod), F32)),
        grid=(n_steps,),
        in_specs=[
            pl.BlockSpec((ts, D_MODEL), row),
            pl.BlockSpec((1, ts), lambda i: (0, i)),
            _const_spec(mod1.shape),
            _const_spec(g1.shape),
            _const_spec(w_in.shape),
            _const_spec(invf.shape),
            _const_spec(dmask.shape),
            _const_spec(xi.shape),
            _const_spec(zeta.shape),
            _const_spec(rdecay.shape),
            _const_spec(rmask.shape),
            _const_spec(retg.shape),
            _const_spec(scw.shape),
            _const_spec(cng.shape),
            _const_spec(w_out.shape),
            wup_spec,
            wdown_spec,
            pl.BlockSpec((c_rows, 1), row),
            pl.BlockSpec((c_rows, n_mod2), lambda i: (i, 1)),
            pl.BlockSpec((1, n_mod2), lambda i: (0, 1)),
            pl.BlockSpec((c_rows, n_fmod), row),
            pl.BlockSpec((1, n_fmod), lambda i: (0, 0)),
        ],
        out_specs=(pl.BlockSpec((ts, D_MODEL), row), wup_spec, wdown_spec,
                   pl.BlockSpec((1, n_mod2), lambda i: (0, 0)),
                   pl.BlockSpec((1, n_fmod), lambda i: (0, 0))),
        scratch_shapes=[
            pltpu.VMEM((ts, RET_WIDTH), BF16),
            pltpu.VMEM((3, RET_WIDTH, ts), BF16),
            pltpu.VMEM((3, ts, RET_WIDTH), BF16),
            pltpu.VMEM((ts, RET_WIDTH), F32),
            pltpu.VMEM((ts + 2 * SUBLANES, CONV_WIDTH), F32),
            pltpu.VMEM((ts + 2 * SUBLANES, CONV_WIDTH), F32),
            pltpu.VMEM((HEAD_PAIRS, LANES, LANES), F32),
            pltpu.VMEM(w_in.shape, BF16),
            pltpu.VMEM(w_out.shape, BF16),
        ],
        compiler_params=pltpu.CompilerParams(
            dimension_semantics=("arbitrary",), vmem_limit_bytes=VMEM_LIMIT_BYTES),
        name="token_mix",
    )(x, pos_row, mod1, g1, w_in, invf, dmask, xi, zeta, rdecay, rmask, retg, scw, cng, w_out,
      w_up, w_down, c_col, mod_w, mod_b, fmod_w, fmod_b)


def _ffn_kernel(x_ref, mod_ref, fmod_ref, g2_ref, wup_ref, fcw_ref, fcb_ref, wdown_ref, fg_ref,
                o_ref, a1_s, a2_s, act_s):
    ts = x_ref.shape[0]
    _delay_init(a1_s, a2_s, ts, pl.program_id(0) == 0)

    shift2 = mod_ref[:, 0:D_MODEL]
    scale2 = mod_ref[:, D_MODEL:2 * D_MODEL]
    gate2 = mod_ref[:, 2 * D_MODEL:3 * D_MODEL]
    final_shift = fmod_ref[:, 0:D_MODEL]
    final_scale = fmod_ref[:, D_MODEL:2 * D_MODEL]
    r0 = 0
    for nrows in FFN_ROW_BLOCKS:
        rows = slice(r0, r0 + nrows)
        x = x_ref[rows, :]
        h = _rms_modulate(x, g2_ref[...], scale2, shift2).astype(BF16)
        for lo in range(0, D_FF, FFN_COL_CHUNK):
            cols = slice(lo, lo + FFN_COL_CHUNK)
            a0 = _wdot(h, wup_ref[:, lo:lo + FFN_COL_CHUNK])
            val = _wdot(h, wup_ref[:, D_FF + lo:D_FF + lo + FFN_COL_CHUNK])
            a = _causal_conv3(a1_s, a2_s, a0, fcw_ref[:, cols], r0, cols) + fcb_ref[:, cols]
            act_s[rows, cols] = (_silu(a) * val).astype(BF16)
        ffn = _wdot(act_s[rows, :], wdown_ref[...])
        x2 = x + gate2 * ffn
        o_ref[rows, :] = _rms_modulate(x2, fg_ref[...], final_scale, final_shift)
        r0 += nrows


def _ffn_call(x, mod, fmod, g2, w_up, fcw, fcb, w_down, fg):
    s = x.shape[0]
    ts = TS_FFN
    row = lambda i: (i, 0)
    return pl.pallas_call(
        _ffn_kernel,
        out_shape=jax.ShapeDtypeStruct((s, D_MODEL), F32),
        grid=(s // ts,),
        in_specs=[
            pl.BlockSpec((ts, D_MODEL), row),
            _const_spec(mod.shape),
            _const_spec(fmod.shape),
            _const_spec(g2.shape),
            _const_spec(w_up.shape),
            _const_spec(fcw.shape),
            _const_spec(fcb.shape),
            _const_spec(w_down.shape),
            _const_spec(fg.shape),
        ],
        out_specs=pl.BlockSpec((ts, D_MODEL), row),
        scratch_shapes=[
            pltpu.VMEM((ts + 2 * SUBLANES, D_FF), F32),
            pltpu.VMEM((ts + 2 * SUBLANES, D_FF), F32),
            pltpu.VMEM((ts, D_FF), BF16),
        ],
        compiler_params=pltpu.CompilerParams(
            dimension_semantics=("arbitrary",), vmem_limit_bytes=VMEM_LIMIT_BYTES),
        name="convglu_ffn",
    )(x, mod, fmod, g2, w_up, fcw, fcb, w_down, fg)


def kernel(x, c, positions, mod_w, mod_b, norm1_g, w_in, ret_norm_g, short_conv_w, conv_norm_g,
           w_out, norm2_g, w_up, ffn_conv_w, ffn_conv_b, w_down, final_mod_w, final_mod_b,
           final_norm_g):
    b, s, d = x.shape
    assert (b, d) == (1, D_MODEL) and mod_w.shape[0] == 1 and s % max(TS_MIX, TS_FFN) == 0
    c_col = c.reshape(d, 1)
    mod1 = _mod_call(c_col, mod_w[0], mod_b, 3 * d)

    x2d = x.reshape(s, d)
    x1, w_up_bf, w_down_bf, mod2, fmod = _mix_call(
        x2d, positions, mod1, norm1_g, w_in[0], ret_norm_g, short_conv_w[0], conv_norm_g, w_out[0],
        w_up[0], w_down[0], c_col, mod_w[0], mod_b, final_mod_w, final_mod_b[None, :])
    out = _ffn_call(x1, mod2, fmod, norm2_g, w_up_bf, ffn_conv_w[0], ffn_conv_b, w_down_bf,
                    final_norm_g[None, :])
    return out.reshape(b, s, d)
```

```python
import numpy as np
import jax
import jax.numpy as jnp
from jax import lax
from jax.experimental import pallas as pl
from jax.experimental.pallas import tpu as pltpu

F32 = jnp.float32
BF16 = jnp.bfloat16

D_MODEL = 1024
RET_WIDTH = 512
RET_HEADS = 8
HEAD_DIM = 64
CONV_WIDTH = 512
GROUP_DIM = 64
CHUNK = 128
D_FF = 2816
ROPE_BASE = 10000.0
EPS = 1e-6
IN_COLS = 4 * RET_WIDTH + 3 * CONV_WIDTH

LANES = 128
SUBLANES = 8
HEAD_PAIRS = RET_WIDTH // LANES
VMEM_LIMIT_BYTES = 56 * 1024 * 1024

TS_MIX = 512
TS_FFN = 512
FFN_COL_CHUNK = 256
FFN_DOWN_BLOCKS = (256, 256)


def _const_spec(shape):
    return pl.BlockSpec(shape, lambda i: (0,) * len(shape), pipeline_mode=pl.Buffered(1))


def _rms_modulate(x, gain, scale, shift):
    ms = jnp.mean(x * x, axis=-1, keepdims=True)
    return x * lax.rsqrt(ms + EPS) * (gain * (1.0 + scale)) + shift


def _silu(x):
    return x * jax.nn.sigmoid(x)


def _wdot(act, w):
    return lax.dot_general(act, w, (((1,), (0,)), ((), ())), preferred_element_type=F32)


MOD_ROWS = 512


def _mod_kernel(c_ref, w_ref, b_ref, o_ref):
    @pl.when(pl.program_id(0) == 0)
    def _():
        o_ref[...] = b_ref[...]

    o_ref[...] += jnp.sum(_silu(c_ref[...]) * w_ref[...], axis=0, keepdims=True)


def _mod_call(c_col, w, b_row, n_cols):
    d = w.shape[0]
    return pl.pallas_call(
        _mod_kernel,
        out_shape=jax.ShapeDtypeStruct((1, n_cols), F32),
        grid=(d // MOD_ROWS,),
        in_specs=[
            pl.BlockSpec((MOD_ROWS, 1), lambda j: (j, 0)),
            pl.BlockSpec((MOD_ROWS, n_cols), lambda j: (j, 0)),
            pl.BlockSpec((1, n_cols), lambda j: (0, 0)),
        ],
        out_specs=pl.BlockSpec((1, n_cols), lambda j: (0, 0)),
        compiler_params=pltpu.CompilerParams(
            dimension_semantics=("arbitrary",), vmem_limit_bytes=VMEM_LIMIT_BYTES),
        name="adaln_mod",
    )(c_col, w, b_row)


def _group_rms(y, gain):
    lane = lax.broadcasted_iota(jnp.int32, (y.shape[0], LANES), 1)
    group0 = (lane & GROUP_DIM) == 0
    outs = []
    for lo in range(0, y.shape[1], LANES):
        yy = y[:, lo:lo + LANES]
        y2 = yy * yy
        s0 = jnp.sum(jnp.where(group0, y2, 0.0), axis=-1, keepdims=True)
        s1 = jnp.sum(jnp.where(group0, 0.0, y2), axis=-1, keepdims=True)
        ss = jnp.where(group0, s0, s1)
        outs.append(yy * lax.rsqrt(ss * (1.0 / GROUP_DIM) + EPS))
    return jnp.concatenate(outs, axis=1) * gain


def _delay_init(d1_ref, d2_ref, ts, first_step):
    head = slice(SUBLANES, 2 * SUBLANES)
    past = slice(ts + SUBLANES, ts + 2 * SUBLANES)

    @pl.when(first_step)
    def _():
        d1_ref[past, :] = jnp.zeros((SUBLANES, d1_ref.shape[1]), F32)
        d2_ref[past, :] = jnp.zeros((SUBLANES, d2_ref.shape[1]), F32)

    d1_ref[head, :] = d1_ref[past, :]
    d2_ref[head, :] = d2_ref[past, :]


def _causal_conv3(d1_ref, d2_ref, u, w, row0, cols):
    n = u.shape[0]
    lo = SUBLANES + row0
    d1_ref[lo + 1:lo + n + 1, cols] = u
    d2_ref[lo + 2:lo + n + 2, cols] = u
    return (d2_ref[lo:lo + n, cols] * w[0:1, :] + d1_ref[lo:lo + n, cols] * w[1:2, :]
            + u * w[2:3, :])


def _mix_kernel(x_ref, pos_ref, mod_ref, g1_ref, win_ref, invf_ref, dmask_ref, xi_ref, zeta_ref,
                rdecay_ref, rmask_ref, retg_ref, scw_ref, cng_ref, wout_ref, wup_ref, wdown_ref,
                c_ref, modw_ref, modb_ref, fmodw_ref, fmodb_ref,
                o_ref, wup_bf_ref, wdown_bf_ref, mod2_ref, fmod_ref, q_s, kt_s, v_s, y_s, u1_s, u2_s,
                r_s, win_s, wout_s):
    ts = x_ref.shape[0]
    _delay_init(u1_s, u2_s, ts, pl.program_id(0) == 0)

    @pl.when(pl.program_id(0) == 0)
    def _():
        r_s[...] = jnp.zeros_like(r_s)
        mod2_ref[...] = modb_ref[...]
        fmod_ref[...] = fmodb_ref[...]
        for col in range(0, IN_COLS, RET_WIDTH):
            win_s[:, col:col + RET_WIDTH] = win_ref[:, col:col + RET_WIDTH].astype(BF16)
        wout_s[...] = wout_ref[...].astype(BF16)

    x = x_ref[...]
    shift1 = mod_ref[:, 0:D_MODEL]
    scale1 = mod_ref[:, D_MODEL:2 * D_MODEL]
    gate1 = mod_ref[:, 2 * D_MODEL:3 * D_MODEL]
    h = _rms_modulate(x, g1_ref[...], scale1, shift1).astype(BF16)

    def proj(col):
        return _wdot(h, win_s[:, col:col + RET_WIDTH])

    half = HEAD_DIM // 2
    ang_t = invf_ref[...] * pos_ref[...].astype(F32)
    cos_t = jnp.cos(ang_t)
    sin_t = jnp.sin(ang_t)
    trig = jnp.concatenate([cos_t, sin_t, cos_t, sin_t], axis=0).T
    lane = lax.broadcasted_iota(jnp.int32, (ts, LANES), 1)
    first_half = (lane & half) == 0
    cos = jnp.where(first_half, trig, pltpu.roll(trig, half, 1))
    sin_signed = jnp.where(first_half, -pltpu.roll(trig, LANES - half, 1), trig)

    def rotary(t):
        swapped = jnp.where(first_half, pltpu.roll(t, LANES - half, 1), pltpu.roll(t, half, 1))
        return t * cos + swapped * sin_signed

    q = proj(0)
    k = proj(RET_WIDTH)
    k_rot = []
    for p in range(HEAD_PAIRS):
        ln = slice(p * LANES, (p + 1) * LANES)
        q_s[:, ln] = rotary(q[:, ln]).astype(BF16)
        k_rot.append(rotary(k[:, ln]) * (HEAD_DIM ** -0.5))
    k_t = jnp.concatenate(k_rot, axis=1).T
    dim_head0 = (lax.broadcasted_iota(jnp.int32, k_t.shape, 0) & HEAD_DIM) == 0
    kt_s[0] = k_t.astype(BF16)
    kt_s[1] = jnp.where(dim_head0, k_t, 0.0).astype(BF16)
    kt_s[2] = jnp.where(dim_head0, 0.0, k_t).astype(BF16)
    v = proj(2 * RET_WIDTH)
    lane_head0 = (lax.broadcasted_iota(jnp.int32, v.shape, 1) & HEAD_DIM) == 0
    v_s[0] = v.astype(BF16)
    v_s[1] = jnp.where(lane_head0, v, 0.0).astype(BF16)
    v_s[2] = jnp.where(lane_head0, 0.0, v).astype(BF16)

    for c in range(ts // CHUNK):
        rows = slice(c * CHUNK, (c + 1) * CHUNK)
        for p in range(HEAD_PAIRS):
            ln = slice(p * LANES, (p + 1) * LANES)
            qc = q_s[rows, ln]
            kcat = jnp.concatenate([kt_s[1, ln, rows], kt_s[2, ln, rows]], axis=1)
            scores = jnp.dot(qc, kcat, preferred_element_type=F32) * dmask_ref[p]
            vcat = jnp.concatenate([v_s[1, rows, ln], v_s[2, rows, ln]], axis=0)
            y_inner = jnp.dot(scores.astype(BF16), vcat, preferred_element_type=F32)
            r = r_s[p]
            y_cross = jnp.dot(qc, r.astype(BF16), preferred_element_type=F32) * xi_ref[p]
            y_s[rows, ln] = y_inner + y_cross
            vz = (v_s[0, rows, ln].astype(F32) * zeta_ref[p]).astype(BF16)
            kv = jnp.dot(kt_s[0, ln, rows], vz, preferred_element_type=F32)
            r_s[p] = r * rdecay_ref[p] + kv * rmask_ref[p]

    y_ret = _group_rms(y_s[...], retg_ref[...]) * _silu(proj(3 * RET_WIDTH))

    u = proj(4 * RET_WIDTH + CONV_WIDTH) * proj(4 * RET_WIDTH + 2 * CONV_WIDTH)
    y_conv = proj(4 * RET_WIDTH) * _causal_conv3(u1_s, u2_s, u, scw_ref[...], 0, slice(None))
    y_conv = _group_rms(y_conv, cng_ref[...])

    cat =jnp.concatenate([y_ret.astype(BF16), y_conv.astype(BF16)], axis=1)
    mixed = _wdot(cat, wout_s[...])
    o_ref[...] = x + gate1 * mixed

    wup_bf_ref[...] = wup_ref[...].astype(BF16)
    wdown_bf_ref[...] = wdown_ref[...].astype(BF16)
    ca = _silu(c_ref[...])
    mod2_ref[...] += jnp.sum(ca * modw_ref[...], axis=0, keepdims=True)
    fmod_ref[...] += jnp.sum(ca * fmodw_ref[...], axis=0, keepdims=True)


def _retention_tables():
    h = np.arange(RET_HEADS, dtype=np.float64)
    log_gamma = np.log1p(-np.exp2(-5.0 - h))
    idx = np.arange(CHUNK, dtype=np.float64)
    rel = idx[:, None] - idx[None, :]
    dmask = np.where(rel[None] >= 0.0,
                     np.exp(np.maximum(rel, 0.0)[None] * log_gamma[:, None, None]), 0.0)
    dmask = dmask.reshape(HEAD_PAIRS, 2, CHUNK, CHUNK).transpose(0, 2, 1, 3)
    dmask = dmask.reshape(HEAD_PAIRS, CHUNK, 2 * CHUNK)
    zeta = np.exp((CHUNK - 1.0 - idx)[None, :] * log_gamma[:, None])
    xi = np.exp((idx + 1.0)[None, :] * log_gamma[:, None])
    chunk_decay = np.exp(CHUNK * log_gamma)

    def per_lane(t):
        t = t.reshape(HEAD_PAIRS, 2, CHUNK)
        return np.repeat(np.transpose(t, (0, 2, 1)), HEAD_DIM, axis=2)

    lane_head = np.arange(LANES) // HEAD_DIM
    rmask = np.broadcast_to(lane_head[:, None] == lane_head[None, :], (HEAD_PAIRS, LANES, LANES))
    rdecay = np.repeat(chunk_decay.reshape(HEAD_PAIRS, 1, 2), HEAD_DIM, axis=2)
    rdecay = np.broadcast_to(rdecay, (HEAD_PAIRS, LANES, LANES))
    return tuple(jnp.asarray(np.ascontiguousarray(t), dtype=F32)
                 for t in (dmask, per_lane(xi), per_lane(zeta), rdecay, rmask))


def _side_cast_spec(w, n_steps):
    k, n = w.shape
    nblk = next(b for b in range(n_steps, 0, -1)
                if k % b == 0 and (k // b) % (2 * SUBLANES) == 0)
    return pl.BlockSpec((k // nblk, n), lambda i: (jnp.minimum(i, nblk - 1), 0))


def _mix_call(x, pos_row, mod1, g1, w_in, retg, scw, cng, w_out, w_up, w_down,
              c_col, mod_w, mod_b, fmod_w, fmod_b):
    s = x.shape[0]
    ts = TS_MIX
    n_steps = s // ts
    half = HEAD_DIM // 2
    invf = (ROPE_BASE ** (-jnp.arange(half, dtype=F32) / half))[:, None]
    dmask, xi, zeta, rdecay, rmask = _retention_tables()
    row = lambda i: (i, 0)
    wup_spec = _side_cast_spec(w_up, n_steps)
    wdown_spec = _side_cast_spec(w_down, n_steps)
    n_mod2 = mod1.shape[1]
    assert mod_w.shape[1] == 2 * n_mod2
    n_fmod = fmod_w.shape[1]
    c_rows = c_col.shape[0] // n_steps
    return pl.pallas_call(
        _mix_kernel,
        out_shape=(jax.ShapeDtypeStruct((s, D_MODEL), F32),
                   jax.ShapeDtypeStruct(w_up.shape, BF16),
                   jax.ShapeDtypeStruct(w_down.shape, BF16),
                   jax.ShapeDtypeStruct((1, n_mod2), F32),
                   jax.ShapeDtypeStruct((1, n_fmod), F32)),
        grid=(n_steps,),
        in_specs=[
            pl.BlockSpec((ts, D_MODEL), row),
            pl.BlockSpec((1, ts), lambda i: (0, i)),
            _const_spec(mod1.shape),
            _const_spec(g1.shape),
            _const_spec(w_in.shape),
            _const_spec(invf.shape),
            _const_spec(dmask.shape),
            _const_spec(xi.shape),
            _const_spec(zeta.shape),
            _const_spec(rdecay.shape),
            _const_spec(rmask.shape),
            _const_spec(retg.shape),
            _const_spec(scw.shape),
            _const_spec(cng.shape),
            _const_spec(w_out.shape),
            wup_spec,
            wdown_spec,
            pl.BlockSpec((c_rows, 1), row),
            pl.BlockSpec((c_rows, n_mod2), lambda i: (i, 1)),
            pl.BlockSpec((1, n_mod2), lambda i: (0, 1)),
            pl.BlockSpec((c_rows, n_fmod), row),
            pl.BlockSpec((1, n_fmod), lambda i: (0, 0)),
        ],
        out_specs=(pl.BlockSpec((ts, D_MODEL), row), wup_spec, wdown_spec,
                   pl.BlockSpec((1, n_mod2), lambda i: (0, 0)),
                   pl.BlockSpec((1, n_fmod), lambda i: (0, 0))),
        scratch_shapes=[
            pltpu.VMEM((ts, RET_WIDTH), BF16),
            pltpu.VMEM((3, RET_WIDTH, ts), BF16),
            pltpu.VMEM((3, ts, RET_WIDTH), BF16),
            pltpu.VMEM((ts, RET_WIDTH), F32),
            pltpu.VMEM((ts + 2 * SUBLANES, CONV_WIDTH), F32),
            pltpu.VMEM((ts + 2 * SUBLANES, CONV_WIDTH), F32),
            pltpu.VMEM((HEAD_PAIRS, LANES, LANES), F32),
            pltpu.VMEM(w_in.shape, BF16),
            pltpu.VMEM(w_out.shape, BF16),
        ],
        compiler_params=pltpu.CompilerParams(
            dimension_semantics=("arbitrary",), vmem_limit_bytes=VMEM_LIMIT_BYTES),
        name="token_mix",
    )(x, pos_row, mod1, g1, w_in, invf, dmask, xi, zeta, rdecay, rmask, retg, scw, cng, w_out,
      w_up, w_down, c_col, mod_w, mod_b, fmod_w, fmod_b)


def _ffn_kernel(x_ref, mod_ref, fmod_ref, g2_ref, wup_ref, fcw_ref, fcb_ref, wdown_ref, fg_ref,
                o_ref, a1_s, a2_s, act_s):
    ts = x_ref.shape[0]
    _delay_init(a1_s, a2_s, ts, pl.program_id(0) == 0)

    shift2 = mod_ref[:, 0:D_MODEL]
    scale2 = mod_ref[:, D_MODEL:2 * D_MODEL]
    gate2 = mod_ref[:, 2 * D_MODEL:3 * D_MODEL]
    final_shift = fmod_ref[:, 0:D_MODEL]
    final_scale = fmod_ref[:, D_MODEL:2 * D_MODEL]
    x = x_ref[...]
    h = _rms_modulate(x, g2_ref[...], scale2, shift2).astype(BF16)
    for lo in range(0, D_FF, FFN_COL_CHUNK):
        cols = slice(lo, lo + FFN_COL_CHUNK)
        a0 = _wdot(h, wup_ref[:, lo:lo + FFN_COL_CHUNK])
        val = _wdot(h, wup_ref[:, D_FF + lo:D_FF + lo + FFN_COL_CHUNK])
        a = _causal_conv3(a1_s, a2_s, a0, fcw_ref[:, cols], 0, cols) + fcb_ref[:, cols]
        act_s[:, cols] = (_silu(a) * val).astype(BF16)
    r0 = 0
    for nrows in FFN_DOWN_BLOCKS:
        rows = slice(r0, r0 + nrows)
        r0 += nrows
        ffn = _wdot(act_s[rows, :], wdown_ref[...])
        x2 = x[rows] + gate2 * ffn
        o_ref[rows, :] = _rms_modulate(x2, fg_ref[...], final_scale, final_shift)


def _ffn_call(x, mod, fmod, g2, w_up, fcw, fcb, w_down, fg):
    s = x.shape[0]
    ts = TS_FFN
    row = lambda i: (i, 0)
    return pl.pallas_call(
        _ffn_kernel,
        out_shape=jax.ShapeDtypeStruct((s, D_MODEL), F32),
        grid=(s // ts,),
        in_specs=[
            pl.BlockSpec((ts, D_MODEL), row),
            _const_spec(mod.shape),
            _const_spec(fmod.shape),
            _const_spec(g2.shape),
            _const_spec(w_up.shape),
            _const_spec(fcw.shape),
            _const_spec(fcb.shape),
            _const_spec(w_down.shape),
            _const_spec(fg.shape),
        ],
        out_specs=pl.BlockSpec((ts, D_MODEL), row),
        scratch_shapes=[
            pltpu.VMEM((ts + 2 * SUBLANES, D_FF), F32),
            pltpu.VMEM((ts + 2 * SUBLANES, D_FF), F32),
            pltpu.VMEM((ts, D_FF), BF16),
        ],
        compiler_params=pltpu.CompilerParams(
            dimension_semantics=("arbitrary",), vmem_limit_bytes=VMEM_LIMIT_BYTES),
        name="convglu_ffn",
    )(x, mod, fmod, g2, w_up, fcw, fcb, w_down, fg)


def kernel(x, c, positions, mod_w, mod_b, norm1_g, w_in, ret_norm_g, short_conv_w, conv_norm_g,
           w_out, norm2_g, w_up, ffn_conv_w, ffn_conv_b, w_down, final_mod_w, final_mod_b,
           final_norm_g):
    b, s, d = x.shape
    assert (b, d) == (1, D_MODEL) and mod_w.shape[0] == 1 and s % max(TS_MIX, TS_FFN) == 0
    c_col = c.reshape(d, 1)
    mod1 = _mod_call(c_col, mod_w[0], mod_b, 3 * d)

    x2d = x.reshape(s, d)
    x1, w_up_bf, w_down_bf, mod2, fmod = _mix_call(
        x2d, positions, mod1, norm1_g, w_in[0], ret_norm_g, short_conv_w[0], conv_norm_g, w_out[0],
        w_up[0], w_down[0], c_col, mod_w[0], mod_b, final_mod_w, final_mod_b[None, :])
    out = _ffn_call(x1, mod2, fmod, norm2_g, w_up_bf, ffn_conv_w[0], ffn_conv_b, w_down_bf,
                    final_norm_g[None, :])
    return out.reshape(b, s, d)
```

```python
import numpy as np
import jax
import jax.numpy as jnp
from jax import lax
from jax.experimental import pallas as pl
from jax.experimental.pallas import tpu as pltpu

F32 = jnp.float32
BF16 = jnp.bfloat16

D_MODEL = 1024
RET_WIDTH = 512
RET_HEADS = 8
HEAD_DIM = 64
CONV_WIDTH = 512
GROUP_DIM = 64
CHUNK = 128
D_FF = 2816
ROPE_BASE = 10000.0
EPS = 1e-6
IN_COLS = 4 * RET_WIDTH + 3 * CONV_WIDTH

LANES = 128
SUBLANES = 8
HEAD_PAIRS = RET_WIDTH // LANES
VMEM_LIMIT_BYTES = 56 * 1024 * 1024

TS_MIX = 512
TS_FFN = 512
FFN_COL_CHUNK = 256
FFN_DOWN_BLOCKS = (256, 256)


def _const_spec(shape):
    return pl.BlockSpec(shape, lambda i: (0,) * len(shape), pipeline_mode=pl.Buffered(1))


def _rms_modulate(x, gain, scale, shift):
    ms = jnp.mean(x * x, axis=-1, keepdims=True)
    return x * lax.rsqrt(ms + EPS) * (gain * (1.0 + scale)) + shift


def _silu(x):
    return x * jax.nn.sigmoid(x)


def _wdot(act, w):
    return lax.dot_general(act, w, (((1,), (0,)), ((), ())), preferred_element_type=F32)


MOD_ROWS = 512


def _as_column(row):
    pieces = []
    for lo in range(0, row.shape[1], LANES):
        tile = jnp.broadcast_to(row[:, lo:lo + LANES], (LANES, LANES))
        pieces.append(tile.T[:, 0:1])
    return jnp.concatenate(pieces, axis=0)


def _mod_kernel(c_ref, w_ref, b_ref, o_ref, ccol_ref):
    @pl.when(pl.program_id(0) == 0)
    def _():
        o_ref[...] = b_ref[...]

    c_col = _as_column(c_ref[...])
    ccol_ref[...] = c_col
    o_ref[...] += jnp.sum(_silu(c_col) * w_ref[...], axis=0, keepdims=True)


def _mod_call(c_row, w, b_row, n_cols):
    d = w.shape[0]
    return pl.pallas_call(
        _mod_kernel,
        out_shape=(jax.ShapeDtypeStruct((1, n_cols), F32), jax.ShapeDtypeStruct((d, 1), F32)),
        grid=(d // MOD_ROWS,),
        in_specs=[
            pl.BlockSpec((1, MOD_ROWS), lambda j: (0, j)),
            pl.BlockSpec((MOD_ROWS, n_cols), lambda j: (j, 0)),
            pl.BlockSpec((1, n_cols), lambda j: (0, 0)),
        ],
        out_specs=(pl.BlockSpec((1, n_cols), lambda j: (0, 0)),
                   pl.BlockSpec((MOD_ROWS, 1), lambda j: (j, 0))),
        compiler_params=pltpu.CompilerParams(
            dimension_semantics=("arbitrary",), vmem_limit_bytes=VMEM_LIMIT_BYTES),
        name="adaln_mod",
    )(c_row, w, b_row)


def _group_rms(y, gain):
    lane = lax.broadcasted_iota(jnp.int32, (y.shape[0], LANES), 1)
    group0 = (lane & GROUP_DIM) == 0
    outs = []
    for lo in range(0, y.shape[1], LANES):
        yy = y[:, lo:lo + LANES]
        y2 = yy * yy
        s0 = jnp.sum(jnp.where(group0, y2, 0.0), axis=-1, keepdims=True)
        s1 = jnp.sum(jnp.where(group0, 0.0, y2), axis=-1, keepdims=True)
        ss = jnp.where(group0, s0, s1)
        outs.append(yy * lax.rsqrt(ss * (1.0 / GROUP_DIM) + EPS))
    return jnp.concatenate(outs, axis=1) * gain


def _delay_init(d1_ref, d2_ref, ts, first_step):
    head = slice(SUBLANES, 2 * SUBLANES)
    past = slice(ts + SUBLANES, ts + 2 * SUBLANES)

    @pl.when(first_step)
    def _():
        d1_ref[past, :] = jnp.zeros((SUBLANES, d1_ref.shape[1]), F32)
        d2_ref[past, :] = jnp.zeros((SUBLANES, d2_ref.shape[1]), F32)

    d1_ref[head, :] = d1_ref[past, :]
    d2_ref[head, :] = d2_ref[past, :]


def _causal_conv3(d1_ref, d2_ref, u, w, row0, cols):
    n = u.shape[0]
    lo = SUBLANES + row0
    d1_ref[lo + 1:lo + n + 1, cols] = u
    d2_ref[lo + 2:lo + n + 2, cols] = u
    return (d2_ref[lo:lo + n, cols] * w[0:1, :] + d1_ref[lo:lo + n, cols] * w[1:2, :]
            + u * w[2:3, :])


def _mix_kernel(x_ref, pos_ref, mod_ref, g1_ref, win_ref, invf_ref, dmask_ref, xi_ref, zeta_ref,
                rdecay_ref, rmask_ref, retg_ref, scw_ref, cng_ref, wout_ref, wup_ref, wdown_ref,
                c_ref, modw_ref, modb_ref, fmodw_ref, fmodb_ref,
                o_ref, wup_bf_ref, wdown_bf_ref, mod2_ref, fmod_ref, q_s, kt_s, v_s, y_s, u1_s, u2_s,
                r_s, win_s, wout_s):
    ts = x_ref.shape[0]
    _delay_init(u1_s, u2_s, ts, pl.program_id(0) == 0)

    @pl.when(pl.program_id(0) == 0)
    def _():
        r_s[...] = jnp.zeros_like(r_s)
        mod2_ref[...] = modb_ref[...]
        fmod_ref[...] = fmodb_ref[...][None, :]
        for col in range(0, IN_COLS, RET_WIDTH):
            win_s[:, col:col + RET_WIDTH] = win_ref[:, col:col + RET_WIDTH].astype(BF16)
        wout_s[...] = wout_ref[...].astype(BF16)

    x = x_ref[...]
    shift1 = mod_ref[:, 0:D_MODEL]
    scale1 = mod_ref[:, D_MODEL:2 * D_MODEL]
    gate1 = mod_ref[:, 2 * D_MODEL:3 * D_MODEL]
    h = _rms_modulate(x, g1_ref[...], scale1, shift1).astype(BF16)

    def proj(col):
        return _wdot(h, win_s[:, col:col + RET_WIDTH])

    half = HEAD_DIM // 2
    ang_t = invf_ref[...] * pos_ref[...].astype(F32)
    cos_t = jnp.cos(ang_t)
    sin_t = jnp.sin(ang_t)
    trig = jnp.concatenate([cos_t, sin_t, cos_t, sin_t], axis=0).T
    lane = lax.broadcasted_iota(jnp.int32, (ts, LANES), 1)
    first_half = (lane & half) == 0
    cos = jnp.where(first_half, trig, pltpu.roll(trig, half, 1))
    sin_signed = jnp.where(first_half, -pltpu.roll(trig, LANES - half, 1), trig)

    def rotary(t):
        swapped = jnp.where(first_half, pltpu.roll(t, LANES - half, 1), pltpu.roll(t, half, 1))
        return t * cos + swapped * sin_signed

    q = proj(0)
    k = proj(RET_WIDTH)
    k_rot = []
    for p in range(HEAD_PAIRS):
        ln = slice(p * LANES, (p + 1) * LANES)
        q_s[:, ln] = rotary(q[:, ln]).astype(BF16)
        k_rot.append(rotary(k[:, ln]) * (HEAD_DIM ** -0.5))
    k_t = jnp.concatenate(k_rot, axis=1).T
    dim_head0 = (lax.broadcasted_iota(jnp.int32, k_t.shape, 0) & HEAD_DIM) == 0
    kt_s[0] = k_t.astype(BF16)
    kt_s[1] = jnp.where(dim_head0, k_t, 0.0).astype(BF16)
    kt_s[2] = jnp.where(dim_head0, 0.0, k_t).astype(BF16)
    v = proj(2 * RET_WIDTH)
    lane_head0 = (lax.broadcasted_iota(jnp.int32, v.shape, 1) & HEAD_DIM) == 0
    v_s[0] = v.astype(BF16)
    v_s[1] = jnp.where(lane_head0, v, 0.0).astype(BF16)
    v_s[2] = jnp.where(lane_head0, 0.0, v).astype(BF16)

    for c in range(ts // CHUNK):
        rows = slice(c * CHUNK, (c + 1) * CHUNK)
        for p in range(HEAD_PAIRS):
            ln = slice(p * LANES, (p + 1) * LANES)
            qc = q_s[rows, ln]
            kcat = jnp.concatenate([kt_s[1, ln, rows], kt_s[2, ln, rows]], axis=1)
            scores = jnp.dot(qc, kcat, preferred_element_type=F32) * dmask_ref[p]
            vcat = jnp.concatenate([v_s[1, rows, ln], v_s[2, rows, ln]], axis=0)
            y_inner = jnp.dot(scores.astype(BF16), vcat, preferred_element_type=F32)
            r = r_s[p]
            y_cross = jnp.dot(qc, r.astype(BF16), preferred_element_type=F32) * xi_ref[p]
            y_s[rows, ln] = y_inner + y_cross
            vz = (v_s[0, rows, ln].astype(F32) * zeta_ref[p]).astype(BF16)
            kv = jnp.dot(kt_s[0, ln, rows], vz, preferred_element_type=F32)
            r_s[p] = r * rdecay_ref[p] + kv * rmask_ref[p]

    y_ret = _group_rms(y_s[...], retg_ref[...]) * _silu(proj(3 * RET_WIDTH))

    u = proj(4 * RET_WIDTH + CONV_WIDTH) * proj(4 * RET_WIDTH + 2 * CONV_WIDTH)
    y_conv = proj(4 * RET_WIDTH) * _causal_conv3(u1_s, u2_s, u, scw_ref[...], 0, slice(None))
    y_conv = _group_rms(y_conv, cng_ref[...])

    cat =jnp.concatenate([y_ret.astype(BF16), y_conv.astype(BF16)], axis=1)
    mixed = _wdot(cat, wout_s[...])
    o_ref[...] = x + gate1 * mixed

    wup_bf_ref[...] = wup_ref[...].astype(BF16)
    wdown_bf_ref[...] = wdown_ref[...].astype(BF16)
    ca = _silu(c_ref[...])
    mod2_ref[...] += jnp.sum(ca * modw_ref[...], axis=0, keepdims=True)
    fmod_ref[...] += jnp.sum(ca * fmodw_ref[...], axis=0, keepdims=True)


def _retention_tables():
    h = np.arange(RET_HEADS, dtype=np.float64)
    log_gamma = np.log1p(-np.exp2(-5.0 - h))
    idx = np.arange(CHUNK, dtype=np.float64)
    rel = idx[:, None] - idx[None, :]
    dmask = np.where(rel[None] >= 0.0,
                     np.exp(np.maximum(rel, 0.0)[None] * log_gamma[:, None, None]), 0.0)
    dmask = dmask.reshape(HEAD_PAIRS, 2, CHUNK, CHUNK).transpose(0, 2, 1, 3)
    dmask = dmask.reshape(HEAD_PAIRS, CHUNK, 2 * CHUNK)
    zeta = np.exp((CHUNK - 1.0 - idx)[None, :] * log_gamma[:, None])
    xi = np.exp((idx + 1.0)[None, :] * log_gamma[:, None])
    chunk_decay = np.exp(CHUNK * log_gamma)

    def per_lane(t):
        t = t.reshape(HEAD_PAIRS, 2, CHUNK)
        return np.repeat(np.transpose(t, (0, 2, 1)), HEAD_DIM, axis=2)

    lane_head = np.arange(LANES) // HEAD_DIM
    rmask = np.broadcast_to(lane_head[:, None] == lane_head[None, :], (HEAD_PAIRS, LANES, LANES))
    rdecay = np.repeat(chunk_decay.reshape(HEAD_PAIRS, 1, 2), HEAD_DIM, axis=2)
    rdecay = np.broadcast_to(rdecay, (HEAD_PAIRS, LANES, LANES))
    return tuple(jnp.asarray(np.ascontiguousarray(t), dtype=F32)
                 for t in (dmask, per_lane(xi), per_lane(zeta), rdecay, rmask))


def _side_cast_spec(w, n_steps):
    k, n = w.shape
    nblk = next(b for b in range(n_steps, 0, -1)
                if k % b == 0 and (k // b) % (2 * SUBLANES) == 0)
    return pl.BlockSpec((k // nblk, n), lambda i: (jnp.minimum(i, nblk - 1), 0))


def _mix_call(x, pos_row, mod1, g1, w_in, retg, scw, cng, w_out, w_up, w_down,
              c_col, mod_w, mod_b, fmod_w, fmod_b):
    s = x.shape[0]
    ts = TS_MIX
    n_steps = s // ts
    half = HEAD_DIM // 2
    invf = (ROPE_BASE ** (-jnp.arange(half, dtype=F32) / half))[:, None]
    dmask, xi, zeta, rdecay, rmask = _retention_tables()
    row = lambda i: (i, 0)
    wup_spec = _side_cast_spec(w_up, n_steps)
    wdown_spec = _side_cast_spec(w_down, n_steps)
    n_mod2 = mod1.shape[1]
    assert mod_w.shape[1] == 2 * n_mod2
    n_fmod = fmod_w.shape[1]
    c_rows = c_col.shape[0] // n_steps
    return pl.pallas_call(
        _mix_kernel,
        out_shape=(jax.ShapeDtypeStruct((s, D_MODEL), F32),
                   jax.ShapeDtypeStruct(w_up.shape, BF16),
                   jax.ShapeDtypeStruct(w_down.shape, BF16),
                   jax.ShapeDtypeStruct((1, n_mod2), F32),
                   jax.ShapeDtypeStruct((1, n_fmod), F32)),
        grid=(n_steps,),
        in_specs=[
            pl.BlockSpec((ts, D_MODEL), row),
            pl.BlockSpec((1, ts), lambda i: (0, i)),
            _const_spec(mod1.shape),
            _const_spec(g1.shape),
            _const_spec(w_in.shape),
            _const_spec(invf.shape),
            _const_spec(dmask.shape),
            _const_spec(xi.shape),
            _const_spec(zeta.shape),
            _const_spec(rdecay.shape),
            _const_spec(rmask.shape),
            _const_spec(retg.shape),
            _const_spec(scw.shape),
            _const_spec(cng.shape),
            _const_spec(w_out.shape),
            wup_spec,
            wdown_spec,
            pl.BlockSpec((c_rows, 1), row),
            pl.BlockSpec((c_rows, n_mod2), lambda i: (i, 1)),
            pl.BlockSpec((1, n_mod2), lambda i: (0, 1)),
            pl.BlockSpec((c_rows, n_fmod), row),
            pl.BlockSpec((n_fmod,), lambda i: (0,)),
        ],
        out_specs=(pl.BlockSpec((ts, D_MODEL), row), wup_spec, wdown_spec,
                   pl.BlockSpec((1, n_mod2), lambda i: (0, 0)),
                   pl.BlockSpec((1, n_fmod), lambda i: (0, 0))),
        scratch_shapes=[
            pltpu.VMEM((ts, RET_WIDTH), BF16),
            pltpu.VMEM((3, RET_WIDTH, ts), BF16),
            pltpu.VMEM((3, ts, RET_WIDTH), BF16),
            pltpu.VMEM((ts, RET_WIDTH), F32),
            pltpu.VMEM((ts + 2 * SUBLANES, CONV_WIDTH), F32),
            pltpu.VMEM((ts + 2 * SUBLANES, CONV_WIDTH), F32),
            pltpu.VMEM((HEAD_PAIRS, LANES, LANES), F32),
            pltpu.VMEM(w_in.shape, BF16),
            pltpu.VMEM(w_out.shape, BF16),
        ],
        compiler_params=pltpu.CompilerParams(
            dimension_semantics=("arbitrary",), vmem_limit_bytes=VMEM_LIMIT_BYTES),
        name="token_mix",
    )(x, pos_row, mod1, g1, w_in, invf, dmask, xi, zeta, rdecay, rmask, retg, scw, cng, w_out,
      w_up, w_down, c_col, mod_w, mod_b, fmod_w, fmod_b)


def _ffn_kernel(x_ref, mod_ref, fmod_ref, g2_ref, wup_ref, fcw_ref, fcb_ref, wdown_ref, fg_ref,
                o_ref, a1_s, a2_s, act_s):
    ts = x_ref.shape[0]
    _delay_init(a1_s, a2_s, ts, pl.program_id(0) == 0)

    shift2 = mod_ref[:, 0:D_MODEL]
    scale2 = mod_ref[:, D_MODEL:2 * D_MODEL]
    gate2 = mod_ref[:, 2 * D_MODEL:3 * D_MODEL]
    final_shift = fmod_ref[:, 0:D_MODEL]
    final_scale = fmod_ref[:, D_MODEL:2 * D_MODEL]
    x = x_ref[...]
    h = _rms_modulate(x, g2_ref[...], scale2, shift2).astype(BF16)
    for lo in range(0, D_FF, FFN_COL_CHUNK):
        cols = slice(lo, lo + FFN_COL_CHUNK)
        a0 = _wdot(h, wup_ref[:, lo:lo + FFN_COL_CHUNK])
        val = _wdot(h, wup_ref[:, D_FF + lo:D_FF + lo + FFN_COL_CHUNK])
        a = _causal_conv3(a1_s, a2_s, a0, fcw_ref[:, cols], 0, cols) + fcb_ref[:, cols]
        act_s[:, cols] = (_silu(a) * val).astype(BF16)
    r0 = 0
    for nrows in FFN_DOWN_BLOCKS:
        rows = slice(r0, r0 + nrows)
        r0 += nrows
        ffn = _wdot(act_s[rows, :], wdown_ref[...])
        x2 = x[rows] + gate2 * ffn
        o_ref[rows, :] = _rms_modulate(x2, fg_ref[...][None, :], final_scale, final_shift)


def _ffn_call(x, mod, fmod, g2, w_up, fcw, fcb, w_down, fg):
    s = x.shape[0]
    ts = TS_FFN
    row = lambda i: (i, 0)
    return pl.pallas_call(
        _ffn_kernel,
        out_shape=jax.ShapeDtypeStruct((s, D_MODEL), F32),
        grid=(s // ts,),
        in_specs=[
            pl.BlockSpec((ts, D_MODEL), row),
            _const_spec(mod.shape),
            _const_spec(fmod.shape),
            _const_spec(g2.shape),
            _const_spec(w_up.shape),
            _const_spec(fcw.shape),
            _const_spec(fcb.shape),
            _const_spec(w_down.shape),
            _const_spec(fg.shape),
        ],
        out_specs=pl.BlockSpec((ts, D_MODEL), row),
        scratch_shapes=[
            pltpu.VMEM((ts + 2 * SUBLANES, D_FF), F32),
            pltpu.VMEM((ts + 2 * SUBLANES, D_FF), F32),
            pltpu.VMEM((ts, D_FF), BF16),
        ],
        compiler_params=pltpu.CompilerParams(
            dimension_semantics=("arbitrary",), vmem_limit_bytes=VMEM_LIMIT_BYTES),
        name="convglu_ffn",
    )(x, mod, fmod, g2, w_up, fcw, fcb, w_down, fg)


def kernel(x, c, positions, mod_w, mod_b, norm1_g, w_in, ret_norm_g, short_conv_w, conv_norm_g,
           w_out, norm2_g, w_up, ffn_conv_w, ffn_conv_b, w_down, final_mod_w, final_mod_b,
           final_norm_g):
    b, s, d = x.shape
    assert (b, d) == (1, D_MODEL) and mod_w.shape[0] == 1 and s % max(TS_MIX, TS_FFN) == 0
    mod1, c_col = _mod_call(c, mod_w[0], mod_b, 3 * d)

    x2d = x.reshape(s, d)
    x1, w_up_bf, w_down_bf, mod2, fmod = _mix_call(
        x2d, positions, mod1, norm1_g, w_in[0], ret_norm_g, short_conv_w[0], conv_norm_g, w_out[0],
        w_up[0], w_down[0], c_col, mod_w[0], mod_b, final_mod_w, final_mod_b)
    out = _ffn_call(x1, mod2, fmod, norm2_g, w_up_bf, ffn_conv_w[0], ffn_conv_b, w_down_bf,
                    final_norm_g)
    return out.reshape(b, s, d)
```

```python
import numpy as np
import jax
import jax.numpy as jnp
from jax import lax
from jax.experimental import pallas as pl
from jax.experimental.pallas import tpu as pltpu

F32 = jnp.float32
BF16 = jnp.bfloat16

D_MODEL = 1024
RET_WIDTH = 512
RET_HEADS = 8
HEAD_DIM = 64
CONV_WIDTH = 512
GROUP_DIM = 64
CHUNK = 128
D_FF = 2816
ROPE_BASE = 10000.0
EPS = 1e-6
IN_COLS = 4 * RET_WIDTH + 3 * CONV_WIDTH

LANES = 128
SUBLANES = 8
HEAD_PAIRS = RET_WIDTH // LANES
VMEM_LIMIT_BYTES = 56 * 1024 * 1024

TS_MIX = 512
TS_FFN = 512
FFN_COL_CHUNK = 256
FFN_DOWN_BLOCKS = (256, 256)


def _const_spec(shape):
    return pl.BlockSpec(shape, lambda i: (0,) * len(shape), pipeline_mode=pl.Buffered(1))


def _rms_modulate(x, gain, scale, shift):
    ms = jnp.mean(x * x, axis=-1, keepdims=True)
    return x * lax.rsqrt(ms + EPS) * (gain * (1.0 + scale)) + shift


def _silu(x):
    return x * jax.nn.sigmoid(x)


def _wdot(act, w):
    return lax.dot_general(act, w, (((1,), (0,)), ((), ())), preferred_element_type=F32)


MOD_ROWS = 512


def _as_column(row):
    pieces = []
    for lo in range(0, row.shape[1], LANES):
        tile = jnp.broadcast_to(row[:, lo:lo + LANES], (LANES, LANES))
        pieces.append(tile.T[:, 0:1])
    return jnp.concatenate(pieces, axis=0)


def _mod_kernel(c_ref, w_ref, b_ref, o_ref, ccol_ref):
    @pl.when(pl.program_id(0) == 0)
    def _():
        o_ref[...] = b_ref[...]

    c_col = _as_column(c_ref[...])
    ccol_ref[...] = c_col
    o_ref[...] += jnp.sum(_silu(c_col) * w_ref[...], axis=0, keepdims=True)


def _mod_call(c_row, w, b_row, n_cols):
    d = w.shape[0]
    return pl.pallas_call(
        _mod_kernel,
        out_shape=(jax.ShapeDtypeStruct((1, n_cols), F32), jax.ShapeDtypeStruct((d, 1), F32)),
        grid=(d // MOD_ROWS,),
        in_specs=[
            pl.BlockSpec((1, MOD_ROWS), lambda j: (0, j)),
            pl.BlockSpec((MOD_ROWS, n_cols), lambda j: (j, 0)),
            pl.BlockSpec((1, n_cols), lambda j: (0, 0)),
        ],
        out_specs=(pl.BlockSpec((1, n_cols), lambda j: (0, 0)),
                   pl.BlockSpec((MOD_ROWS, 1), lambda j: (j, 0))),
        compiler_params=pltpu.CompilerParams(
            dimension_semantics=("arbitrary",), vmem_limit_bytes=VMEM_LIMIT_BYTES),
        name="adaln_mod",
    )(c_row, w, b_row)


def _group_rms(y, gain):
    lane = lax.broadcasted_iota(jnp.int32, (y.shape[0], LANES), 1)
    group0 = (lane & GROUP_DIM) == 0
    outs = []
    for lo in range(0, y.shape[1], LANES):
        yy = y[:, lo:lo + LANES]
        y2 = yy * yy
        s0 = jnp.sum(jnp.where(group0, y2, 0.0), axis=-1, keepdims=True)
        s1 = jnp.sum(jnp.where(group0, 0.0, y2), axis=-1, keepdims=True)
        ss = jnp.where(group0, s0, s1)
        outs.append(yy * lax.rsqrt(ss * (1.0 / GROUP_DIM) + EPS))
    return jnp.concatenate(outs, axis=1) * gain


def _delay_init(d1_ref, d2_ref, ts, first_step):
    head = slice(SUBLANES, 2 * SUBLANES)
    past = slice(ts + SUBLANES, ts + 2 * SUBLANES)

    @pl.when(first_step)
    def _():
        d1_ref[past, :] = jnp.zeros((SUBLANES, d1_ref.shape[1]), F32)
        d2_ref[past, :] = jnp.zeros((SUBLANES, d2_ref.shape[1]), F32)

    d1_ref[head, :] = d1_ref[past, :]
    d2_ref[head, :] = d2_ref[past, :]


def _causal_conv3(d1_ref, d2_ref, u, w, row0, cols):
    n = u.shape[0]
    lo = SUBLANES + row0
    d1_ref[lo + 1:lo + n + 1, cols] = u
    d2_ref[lo + 2:lo + n + 2, cols] = u
    return (d2_ref[lo:lo + n, cols] * w[0:1, :] + d1_ref[lo:lo + n, cols] * w[1:2, :]
            + u * w[2:3, :])


def _mix_kernel(x_ref, pos_ref, mod_ref, g1_ref, win_ref, invf_ref, dmask_ref, xi_ref, zeta_ref,
                rdecay_ref, rmask_ref, retg_ref, scw_ref, cng_ref, wout_ref, wup_ref, wdown_ref,
                c_ref, modw_ref, modb_ref, fmodw_ref, fmodb_ref,
                o_ref, wup_bf_ref, wdown_bf_ref, mod2_ref, fmod_ref, q_s, kt_s, v_s, y_s, u1_s, u2_s,
                r_s, win_s, wout_s):
    ts = x_ref.shape[0]
    _delay_init(u1_s, u2_s, ts, pl.program_id(0) == 0)

    @pl.when(pl.program_id(0) == 0)
    def _():
        r_s[...] = jnp.zeros_like(r_s)
        mod2_ref[...] = modb_ref[...]
        fmod_ref[...] = fmodb_ref[...]
        for col in range(0, IN_COLS, RET_WIDTH):
            win_s[:, col:col + RET_WIDTH] = win_ref[:, col:col + RET_WIDTH].astype(BF16)
        wout_s[...] = wout_ref[...].astype(BF16)

    x = x_ref[...]
    shift1 = mod_ref[:, 0:D_MODEL]
    scale1 = mod_ref[:, D_MODEL:2 * D_MODEL]
    gate1 = mod_ref[:, 2 * D_MODEL:3 * D_MODEL]
    h = _rms_modulate(x, g1_ref[...], scale1, shift1).astype(BF16)

    def proj(col):
        return _wdot(h, win_s[:, col:col + RET_WIDTH])

    half = HEAD_DIM // 2
    ang_t = invf_ref[...] * pos_ref[...].astype(F32)
    cos_t = jnp.cos(ang_t)
    sin_t = jnp.sin(ang_t)
    trig = jnp.concatenate([cos_t, sin_t, cos_t, sin_t], axis=0).T
    lane = lax.broadcasted_iota(jnp.int32, (ts, LANES), 1)
    first_half = (lane & half) == 0
    cos = jnp.where(first_half, trig, pltpu.roll(trig, half, 1))
    sin_signed = jnp.where(first_half, -pltpu.roll(trig, LANES - half, 1), trig)

    def rotary(t):
        swapped = jnp.where(first_half, pltpu.roll(t, LANES - half, 1), pltpu.roll(t, half, 1))
        return t * cos + swapped * sin_signed

    q = proj(0)
    k = proj(RET_WIDTH)
    k_rot = []
    for p in range(HEAD_PAIRS):
        ln = slice(p * LANES, (p + 1) * LANES)
        q_s[:, ln] = rotary(q[:, ln]).astype(BF16)
        k_rot.append(rotary(k[:, ln]) * (HEAD_DIM ** -0.5))
    k_t = jnp.concatenate(k_rot, axis=1).T
    dim_head0 = (lax.broadcasted_iota(jnp.int32, k_t.shape, 0) & HEAD_DIM) == 0
    kt_s[0] = k_t.astype(BF16)
    kt_s[1] = jnp.where(dim_head0, k_t, 0.0).astype(BF16)
    kt_s[2] = jnp.where(dim_head0, 0.0, k_t).astype(BF16)
    v = proj(2 * RET_WIDTH)
    lane_head0 = (lax.broadcasted_iota(jnp.int32, v.shape, 1) & HEAD_DIM) == 0
    v_s[0] = v.astype(BF16)
    v_s[1] = jnp.where(lane_head0, v, 0.0).astype(BF16)
    v_s[2] = jnp.where(lane_head0, 0.0, v).astype(BF16)

    for c in range(ts // CHUNK):
        rows = slice(c * CHUNK, (c + 1) * CHUNK)
        for p in range(HEAD_PAIRS):
            ln = slice(p * LANES, (p + 1) * LANES)
            qc = q_s[rows, ln]
            kcat = jnp.concatenate([kt_s[1, ln, rows], kt_s[2, ln, rows]], axis=1)
            scores = jnp.dot(qc, kcat, preferred_element_type=F32) * dmask_ref[p]
            vcat = jnp.concatenate([v_s[1, rows, ln], v_s[2, rows, ln]], axis=0)
            y_inner = jnp.dot(scores.astype(BF16), vcat, preferred_element_type=F32)
            r = r_s[p]
            y_cross = jnp.dot(qc, r.astype(BF16), preferred_element_type=F32) * xi_ref[p]
            y_s[rows, ln] = y_inner + y_cross
            vz = (v_s[0, rows, ln].astype(F32) * zeta_ref[p]).astype(BF16)
            kv = jnp.dot(kt_s[0, ln, rows], vz, preferred_element_type=F32)
            r_s[p] = r * rdecay_ref[p] + kv * rmask_ref[p]

    y_ret = _group_rms(y_s[...], retg_ref[...]) * _silu(proj(3 * RET_WIDTH))

    u = proj(4 * RET_WIDTH + CONV_WIDTH) * proj(4 * RET_WIDTH + 2 * CONV_WIDTH)
    y_conv = proj(4 * RET_WIDTH) * _causal_conv3(u1_s, u2_s, u, scw_ref[0], 0, slice(None))
    y_conv = _group_rms(y_conv, cng_ref[...])

    cat =jnp.concatenate([y_ret.astype(BF16), y_conv.astype(BF16)], axis=1)
    mixed = _wdot(cat, wout_s[...])
    o_ref[...] = x + gate1 * mixed

    wup_bf_ref[...] = wup_ref[...].astype(BF16)
    wdown_bf_ref[...] = wdown_ref[...].astype(BF16)
    ca = _silu(c_ref[...])
    mod2_ref[...] += jnp.sum(ca * modw_ref[...], axis=0, keepdims=True)
    fmod_ref[...] += jnp.sum(ca * fmodw_ref[...], axis=0, keepdims=True)


def _retention_tables():
    h = np.arange(RET_HEADS, dtype=np.float64)
    log_gamma = np.log1p(-np.exp2(-5.0 - h))
    idx = np.arange(CHUNK, dtype=np.float64)
    rel = idx[:, None] - idx[None, :]
    dmask = np.where(rel[None] >= 0.0,
                     np.exp(np.maximum(rel, 0.0)[None] * log_gamma[:, None, None]), 0.0)
    dmask = dmask.reshape(HEAD_PAIRS, 2, CHUNK, CHUNK).transpose(0, 2, 1, 3)
    dmask = dmask.reshape(HEAD_PAIRS, CHUNK, 2 * CHUNK)
    zeta = np.exp((CHUNK - 1.0 - idx)[None, :] * log_gamma[:, None])
    xi = np.exp((idx + 1.0)[None, :] * log_gamma[:, None])
    chunk_decay = np.exp(CHUNK * log_gamma)

    def per_lane(t):
        t = t.reshape(HEAD_PAIRS, 2, CHUNK)
        return np.repeat(np.transpose(t, (0, 2, 1)), HEAD_DIM, axis=2)

    lane_head = np.arange(LANES) // HEAD_DIM
    rmask = np.broadcast_to(lane_head[:, None] == lane_head[None, :], (HEAD_PAIRS, LANES, LANES))
    rdecay = np.repeat(chunk_decay.reshape(HEAD_PAIRS, 1, 2), HEAD_DIM, axis=2)
    rdecay = np.broadcast_to(rdecay, (HEAD_PAIRS, LANES, LANES))
    return tuple(jnp.asarray(np.ascontiguousarray(t), dtype=F32)
                 for t in (dmask, per_lane(xi), per_lane(zeta), rdecay, rmask))


def _side_cast_spec(w, n_steps):
    k, n = w.shape
    nblk = next(b for b in range(n_steps, 0, -1)
                if k % b == 0 and (k // b) % (2 * SUBLANES) == 0)
    return pl.BlockSpec((k // nblk, n), lambda i: (jnp.minimum(i, nblk - 1), 0))


def _mix_call(x, pos_row, mod1, g1, w_in, retg, scw, cng, w_out, w_up, w_down,
              c_col, mod_w, mod_b, fmod_w, fmod_b):
    s = x.shape[0]
    ts = TS_MIX
    n_steps = s // ts
    half = HEAD_DIM // 2
    invf = (ROPE_BASE ** (-jnp.arange(half, dtype=F32) / half))[:, None]
    dmask, xi, zeta, rdecay, rmask = _retention_tables()
    row = lambda i: (i, 0)
    wup_spec = _side_cast_spec(w_up, n_steps)
    wdown_spec = _side_cast_spec(w_down, n_steps)
    n_mod2 = mod1.shape[1]
    assert mod_w.shape[1] == 2 * n_mod2
    n_fmod = fmod_w.shape[1]
    c_rows = c_col.shape[0] // n_steps
    return pl.pallas_call(
        _mix_kernel,
        out_shape=(jax.ShapeDtypeStruct((s, D_MODEL), F32),
                   jax.ShapeDtypeStruct(w_up.shape, BF16),
                   jax.ShapeDtypeStruct(w_down.shape, BF16),
                   jax.ShapeDtypeStruct((1, n_mod2), F32),
                   jax.ShapeDtypeStruct((1, n_fmod), F32)),
        grid=(n_steps,),
        in_specs=[
            pl.BlockSpec((ts, D_MODEL), row),
            pl.BlockSpec((1, ts), lambda i: (0, i)),
            _const_spec(mod1.shape),
            _const_spec(g1.shape),
            _const_spec(w_in.shape),
            _const_spec(invf.shape),
            _const_spec(dmask.shape),
            _const_spec(xi.shape),
            _const_spec(zeta.shape),
            _const_spec(rdecay.shape),
            _const_spec(rmask.shape),
            _const_spec(retg.shape),
            _const_spec(scw.shape),
            _const_spec(cng.shape),
            _const_spec(w_out.shape),
            wup_spec,
            wdown_spec,
            pl.BlockSpec((c_rows, 1), row),
            pl.BlockSpec((c_rows, n_mod2), lambda i: (i, 1)),
            pl.BlockSpec((1, n_mod2), lambda i: (0, 1)),
            pl.BlockSpec((c_rows, n_fmod), row),
            pl.BlockSpec((1, n_fmod), lambda i: (0, 0)),
        ],
        out_specs=(pl.BlockSpec((ts, D_MODEL), row), wup_spec, wdown_spec,
                   pl.BlockSpec((1, n_mod2), lambda i: (0, 0)),
                   pl.BlockSpec((1, n_fmod), lambda i: (0, 0))),
        scratch_shapes=[
            pltpu.VMEM((ts, RET_WIDTH), BF16),
            pltpu.VMEM((3, RET_WIDTH, ts), BF16),
            pltpu.VMEM((3, ts, RET_WIDTH), BF16),
            pltpu.VMEM((ts, RET_WIDTH), F32),
            pltpu.VMEM((ts + 2 * SUBLANES, CONV_WIDTH), F32),
            pltpu.VMEM((ts + 2 * SUBLANES, CONV_WIDTH), F32),
            pltpu.VMEM((HEAD_PAIRS, LANES, LANES), F32),
            pltpu.VMEM(w_in.shape, BF16),
            pltpu.VMEM(w_out.shape, BF16),
        ],
        compiler_params=pltpu.CompilerParams(
            dimension_semantics=("arbitrary",), vmem_limit_bytes=VMEM_LIMIT_BYTES),
        name="token_mix",
    )(x, pos_row, mod1, g1, w_in, invf, dmask, xi, zeta, rdecay, rmask, retg, scw, cng, w_out,
      w_up, w_down, c_col, mod_w, mod_b, fmod_w, fmod_b)


def _ffn_kernel(x_ref, mod_ref, fmod_ref, g2_ref, wup_ref, fcw_ref, fcb_ref, wdown_ref, fg_ref,
                o_ref, a1_s, a2_s, act_s):
    ts = x_ref.shape[0]
    _delay_init(a1_s, a2_s, ts, pl.program_id(0) == 0)

    shift2 = mod_ref[:, 0:D_MODEL]
    scale2 = mod_ref[:, D_MODEL:2 * D_MODEL]
    gate2 = mod_ref[:, 2 * D_MODEL:3 * D_MODEL]
    final_shift = fmod_ref[:, 0:D_MODEL]
    final_scale = fmod_ref[:, D_MODEL:2 * D_MODEL]
    x = x_ref[...]
    h = _rms_modulate(x, g2_ref[...], scale2, shift2).astype(BF16)
    for lo in range(0, D_FF, FFN_COL_CHUNK):
        cols = slice(lo, lo + FFN_COL_CHUNK)
        a0 = _wdot(h, wup_ref[:, lo:lo + FFN_COL_CHUNK])
        val = _wdot(h, wup_ref[:, D_FF + lo:D_FF + lo + FFN_COL_CHUNK])
        a = _causal_conv3(a1_s, a2_s, a0, fcw_ref[0, :, cols], 0, cols) + fcb_ref[:, cols]
        act_s[:, cols] = (_silu(a) * val).astype(BF16)
    r0 = 0
    for nrows in FFN_DOWN_BLOCKS:
        rows = slice(r0, r0 + nrows)
        r0 += nrows
        ffn = _wdot(act_s[rows, :], wdown_ref[...])
        x2 = x[rows] + gate2 * ffn
        o_ref[rows, :] = _rms_modulate(x2, fg_ref[...], final_scale, final_shift)


def _ffn_call(x, mod, fmod, g2, w_up, fcw, fcb, w_down, fg):
    s = x.shape[0]
    ts = TS_FFN
    row = lambda i: (i, 0)
    return pl.pallas_call(
        _ffn_kernel,
        out_shape=jax.ShapeDtypeStruct((s, D_MODEL), F32),
        grid=(s // ts,),
        in_specs=[
            pl.BlockSpec((ts, D_MODEL), row),
            _const_spec(mod.shape),
            _const_spec(fmod.shape),
            _const_spec(g2.shape),
            _const_spec(w_up.shape),
            _const_spec(fcw.shape),
            _const_spec(fcb.shape),
            _const_spec(w_down.shape),
            pl.BlockSpec((1, D_MODEL), lambda i: (0, fg.shape[1] // D_MODEL - 1),
                         pipeline_mode=pl.Buffered(1)),
        ],
        out_specs=pl.BlockSpec((ts, D_MODEL), row),
        scratch_shapes=[
            pltpu.VMEM((ts + 2 * SUBLANES, D_FF), F32),
            pltpu.VMEM((ts + 2 * SUBLANES, D_FF), F32),
            pltpu.VMEM((ts, D_FF), BF16),
        ],
        compiler_params=pltpu.CompilerParams(
            dimension_semantics=("arbitrary",), vmem_limit_bytes=VMEM_LIMIT_BYTES),
        name="convglu_ffn",
    )(x, mod, fmod, g2, w_up, fcw, fcb, w_down, fg)


def kernel(x, c, positions, mod_w, mod_b, norm1_g, w_in, ret_norm_g, short_conv_w, conv_norm_g,
           w_out, norm2_g, w_up, ffn_conv_w, ffn_conv_b, w_down, final_mod_w, final_mod_b,
           final_norm_g):
    b, s, d = x.shape
    assert (b, d) == (1, D_MODEL) and mod_w.shape[0] == 1 and s % max(TS_MIX, TS_FFN) == 0
    mod1, c_col = _mod_call(c, mod_w[0], mod_b, 3 * d)

    tail_row = jnp.concatenate([final_mod_b, final_norm_g])[None, :]

    x2d = x.reshape(s, d)
    x1, w_up_bf, w_down_bf, mod2, fmod = _mix_call(
        x2d, positions, mod1, norm1_g, w_in[0], ret_norm_g, short_conv_w, conv_norm_g, w_out[0],
        w_up[0], w_down[0], c_col, mod_w[0], mod_b, final_mod_w, tail_row)
    out = _ffn_call(x1, mod2, fmod, norm2_g, w_up_bf, ffn_conv_w, ffn_conv_b, w_down_bf,
                    tail_row)
    return out.reshape(b, s, d)
```

```python
import numpy as np
import jax
import jax.numpy as jnp
from jax import lax
from jax.experimental import pallas as pl
from jax.experimental.pallas import tpu as pltpu

F32 = jnp.float32
BF16 = jnp.bfloat16

D_MODEL = 1024
RET_WIDTH = 512
RET_HEADS = 8
HEAD_DIM = 64
CONV_WIDTH = 512
GROUP_DIM = 64
CHUNK = 128
D_FF = 2816
ROPE_BASE = 10000.0
EPS = 1e-6
IN_COLS = 4 * RET_WIDTH + 3 * CONV_WIDTH

LANES = 128
SUBLANES = 8
HEAD_PAIRS = RET_WIDTH // LANES
VMEM_LIMIT_BYTES = 56 * 1024 * 1024

TS_MIX = 512
TS_FFN = 512
FFN_COL_CHUNK = 256
FFN_DOWN_BLOCKS = (256, 256)


def _const_spec(shape):
    return pl.BlockSpec(shape, lambda i: (0,) * len(shape), pipeline_mode=pl.Buffered(1))


def _rms_modulate(x, gain, scale, shift):
    ms = jnp.mean(x * x, axis=-1, keepdims=True)
    return x * lax.rsqrt(ms + EPS) * (gain * (1.0 + scale)) + shift


def _silu(x):
    return x * jax.nn.sigmoid(x)


def _wdot(act, w):
    return lax.dot_general(act, w, (((1,), (0,)), ((), ())), preferred_element_type=F32)


MOD_ROWS = 512


def _as_column(row):
    pieces = []
    for lo in range(0, row.shape[1], LANES):
        tile = jnp.broadcast_to(row[:, lo:lo + LANES], (LANES, LANES))
        pieces.append(tile.T[:, 0:1])
    return jnp.concatenate(pieces, axis=0)


def _mod_kernel(c_ref, w_ref, b_ref, o_ref, ccol_ref):
    @pl.when(pl.program_id(0) == 0)
    def _():
        o_ref[...] = b_ref[...]

    c_col = _as_column(c_ref[...])
    ccol_ref[...] = c_col
    o_ref[...] += jnp.sum(_silu(c_col) * w_ref[...], axis=0, keepdims=True)


def _mod_call(c_row, w, b_row, n_cols):
    d = w.shape[0]
    return pl.pallas_call(
        _mod_kernel,
        out_shape=(jax.ShapeDtypeStruct((1, n_cols), F32), jax.ShapeDtypeStruct((d, 1), F32)),
        grid=(d // MOD_ROWS,),
        in_specs=[
            pl.BlockSpec((1, MOD_ROWS), lambda j: (0, j)),
            pl.BlockSpec((MOD_ROWS, n_cols), lambda j: (j, 0)),
            pl.BlockSpec((1, n_cols), lambda j: (0, 0)),
        ],
        out_specs=(pl.BlockSpec((1, n_cols), lambda j: (0, 0)),
                   pl.BlockSpec((MOD_ROWS, 1), lambda j: (j, 0))),
        compiler_params=pltpu.CompilerParams(
            dimension_semantics=("arbitrary",), vmem_limit_bytes=VMEM_LIMIT_BYTES),
        name="adaln_mod",
    )(c_row, w, b_row)


def _group_rms(y, gain):
    lane = lax.broadcasted_iota(jnp.int32, (y.shape[0], LANES), 1)
    group0 = (lane & GROUP_DIM) == 0
    outs = []
    for lo in range(0, y.shape[1], LANES):
        yy = y[:, lo:lo + LANES]
        y2 = yy * yy
        s0 = jnp.sum(jnp.where(group0, y2, 0.0), axis=-1, keepdims=True)
        s1 = jnp.sum(jnp.where(group0, 0.0, y2), axis=-1, keepdims=True)
        ss = jnp.where(group0, s0, s1)
        outs.append(yy * lax.rsqrt(ss * (1.0 / GROUP_DIM) + EPS))
    return jnp.concatenate(outs, axis=1) * gain


def _delay_init(d1_ref, d2_ref, ts, first_step):
    head = slice(SUBLANES, 2 * SUBLANES)
    past = slice(ts + SUBLANES, ts + 2 * SUBLANES)

    @pl.when(first_step)
    def _():
        d1_ref[past, :] = jnp.zeros((SUBLANES, d1_ref.shape[1]), F32)
        d2_ref[past, :] = jnp.zeros((SUBLANES, d2_ref.shape[1]), F32)

    d1_ref[head, :] = d1_ref[past, :]
    d2_ref[head, :] = d2_ref[past, :]


def _causal_conv3(d1_ref, d2_ref, u, w, row0, cols):
    n = u.shape[0]
    lo = SUBLANES + row0
    d1_ref[lo + 1:lo + n + 1, cols] = u
    d2_ref[lo + 2:lo + n + 2, cols] = u
    return (d2_ref[lo:lo + n, cols] * w[0:1, :] + d1_ref[lo:lo + n, cols] * w[1:2, :]
            + u * w[2:3, :])


def _mix_kernel(x_ref, pos_ref, mod_ref, g1_ref, win_ref, invf_ref, dmask_ref, xi_ref, zeta_ref,
                rdecay_ref, rmask_ref, retg_ref, scw_ref, cng_ref, wout_ref, wup_ref, wdown_ref,
                c_ref, modw_ref, modb_ref, fmodw_ref, fmodb_ref,
                o_ref, wup_bf_ref, wdown_bf_ref, mod2_ref, fmod_ref, q_s, kt_s, v_s, y_s, u1_s, u2_s,
                r_s, win_s, wout_s):
    ts = x_ref.shape[0]
    _delay_init(u1_s, u2_s, ts, pl.program_id(0) == 0)

    @pl.when(pl.program_id(0) == 0)
    def _():
        r_s[...] = jnp.zeros_like(r_s)
        mod2_ref[...] = modb_ref[...]
        fmod_ref[...] = fmodb_ref[...][None, :]
        for col in range(0, IN_COLS, RET_WIDTH):
            win_s[:, col:col + RET_WIDTH] = win_ref[:, col:col + RET_WIDTH].astype(BF16)
        wout_s[...] = wout_ref[...].astype(BF16)

    x = x_ref[...]
    shift1 = mod_ref[:, 0:D_MODEL]
    scale1 = mod_ref[:, D_MODEL:2 * D_MODEL]
    gate1 = mod_ref[:, 2 * D_MODEL:3 * D_MODEL]
    h = _rms_modulate(x, g1_ref[...], scale1, shift1).astype(BF16)

    def proj(col):
        return _wdot(h, win_s[:, col:col + RET_WIDTH])

    half = HEAD_DIM // 2
    ang_t = invf_ref[...] * pos_ref[...].astype(F32)
    cos_t = jnp.cos(ang_t)
    sin_t = jnp.sin(ang_t)
    trig = jnp.concatenate([cos_t, sin_t, cos_t, sin_t], axis=0).T
    lane = lax.broadcasted_iota(jnp.int32, (ts, LANES), 1)
    first_half = (lane & half) == 0
    cos = jnp.where(first_half, trig, pltpu.roll(trig, half, 1))
    sin_signed = jnp.where(first_half, -pltpu.roll(trig, LANES - half, 1), trig)

    def rotary(t):
        swapped = jnp.where(first_half, pltpu.roll(t, LANES - half, 1), pltpu.roll(t, half, 1))
        return t * cos + swapped * sin_signed

    q = proj(0)
    k = proj(RET_WIDTH)
    k_rot = []
    for p in range(HEAD_PAIRS):
        ln = slice(p * LANES, (p + 1) * LANES)
        q_s[:, ln] = rotary(q[:, ln]).astype(BF16)
        k_rot.append(rotary(k[:, ln]) * (HEAD_DIM ** -0.5))
    k_t = jnp.concatenate(k_rot, axis=1).T
    dim_head0 = (lax.broadcasted_iota(jnp.int32, k_t.shape, 0) & HEAD_DIM) == 0
    kt_s[0] = k_t.astype(BF16)
    kt_s[1] = jnp.where(dim_head0, k_t, 0.0).astype(BF16)
    kt_s[2] = jnp.where(dim_head0, 0.0, k_t).astype(BF16)
    v = proj(2 * RET_WIDTH)
    lane_head0 = (lax.broadcasted_iota(jnp.int32, v.shape, 1) & HEAD_DIM) == 0
    v_s[0] = v.astype(BF16)
    v_s[1] = jnp.where(lane_head0, v, 0.0).astype(BF16)
    v_s[2] = jnp.where(lane_head0, 0.0, v).astype(BF16)

    for c in range(ts // CHUNK):
        rows = slice(c * CHUNK, (c + 1) * CHUNK)
        for p in range(HEAD_PAIRS):
            ln = slice(p * LANES, (p + 1) * LANES)
            qc = q_s[rows, ln]
            kcat = jnp.concatenate([kt_s[1, ln, rows], kt_s[2, ln, rows]], axis=1)
            scores = jnp.dot(qc, kcat, preferred_element_type=F32) * dmask_ref[p]
            vcat = jnp.concatenate([v_s[1, rows, ln], v_s[2, rows, ln]], axis=0)
            y_inner = jnp.dot(scores.astype(BF16), vcat, preferred_element_type=F32)
            r = r_s[p]
            y_cross = jnp.dot(qc, r.astype(BF16), preferred_element_type=F32) * xi_ref[p]
            y_s[rows, ln] = y_inner + y_cross
            vz = (v_s[0, rows, ln].astype(F32) * zeta_ref[p]).astype(BF16)
            kv = jnp.dot(kt_s[0, ln, rows], vz, preferred_element_type=F32)
            r_s[p] = r * rdecay_ref[p] + kv * rmask_ref[p]

    y_ret = _group_rms(y_s[...], retg_ref[...]) * _silu(proj(3 * RET_WIDTH))

    u = proj(4 * RET_WIDTH + CONV_WIDTH) * proj(4 * RET_WIDTH + 2 * CONV_WIDTH)
    y_conv = proj(4 * RET_WIDTH) * _causal_conv3(u1_s, u2_s, u, scw_ref[...], 0, slice(None))
    y_conv = _group_rms(y_conv, cng_ref[...])

    cat = jnp.concatenate([y_ret.astype(BF16), y_conv.astype(BF16)], axis=1)
    mixed = _wdot(cat, wout_s[...])
    o_ref[...] = x + gate1 * mixed

    wup_bf_ref[...] = wup_ref[...].astype(BF16)
    wdown_bf_ref[...] = wdown_ref[...].astype(BF16)
    ca = _silu(c_ref[...])
    mod2_ref[...] += jnp.sum(ca * modw_ref[...], axis=0, keepdims=True)
    fmod_ref[...] += jnp.sum(ca * fmodw_ref[...], axis=0, keepdims=True)


def _retention_tables():
    h = np.arange(RET_HEADS, dtype=np.float64)
    log_gamma = np.log1p(-np.exp2(-5.0 - h))
    idx = np.arange(CHUNK, dtype=np.float64)
    rel = idx[:, None] - idx[None, :]
    dmask = np.where(rel[None] >= 0.0,
                     np.exp(np.maximum(rel, 0.0)[None] * log_gamma[:, None, None]), 0.0)
    dmask = dmask.reshape(HEAD_PAIRS, 2, CHUNK, CHUNK).transpose(0, 2, 1, 3)
    dmask = dmask.reshape(HEAD_PAIRS, CHUNK, 2 * CHUNK)
    zeta = np.exp((CHUNK - 1.0 - idx)[None, :] * log_gamma[:, None])
    xi = np.exp((idx + 1.0)[None, :] * log_gamma[:, None])
    chunk_decay = np.exp(CHUNK * log_gamma)

    def per_lane(t):
        t = t.reshape(HEAD_PAIRS, 2, CHUNK)
        return np.repeat(np.transpose(t, (0, 2, 1)), HEAD_DIM, axis=2)

    lane_head = np.arange(LANES) // HEAD_DIM
    rmask = np.broadcast_to(lane_head[:, None] == lane_head[None, :], (HEAD_PAIRS, LANES, LANES))
    rdecay = np.repeat(chunk_decay.reshape(HEAD_PAIRS, 1, 2), HEAD_DIM, axis=2)
    rdecay = np.broadcast_to(rdecay, (HEAD_PAIRS, LANES, LANES))
    return tuple(jnp.asarray(np.ascontiguousarray(t), dtype=F32)
                 for t in (dmask, per_lane(xi), per_lane(zeta), rdecay, rmask))


def _side_cast_spec(w, n_steps):
    k, n = w.shape
    nblk = next(b for b in range(n_steps, 0, -1)
                if k % b == 0 and (k // b) % (2 * SUBLANES) == 0)
    return pl.BlockSpec((k // nblk, n), lambda i: (jnp.minimum(i, nblk - 1), 0))


def _mix_call(x, pos_row, mod1, g1, w_in, retg, scw, cng, w_out, w_up, w_down,
              c_col, mod_w, mod_b, fmod_w, fmod_b):
    s = x.shape[0]
    ts = TS_MIX
    n_steps = s // ts
    half = HEAD_DIM // 2
    invf = (ROPE_BASE ** (-jnp.arange(half, dtype=F32) / half))[:, None]
    dmask, xi, zeta, rdecay, rmask = _retention_tables()
    row = lambda i: (i, 0)
    wup_spec = _side_cast_spec(w_up, n_steps)
    wdown_spec = _side_cast_spec(w_down, n_steps)
    n_mod2 = mod1.shape[1]
    assert mod_w.shape[1] == 2 * n_mod2
    n_fmod = fmod_w.shape[1]
    c_rows = c_col.shape[0] // n_steps
    return pl.pallas_call(
        _mix_kernel,
        out_shape=(jax.ShapeDtypeStruct((s, D_MODEL), F32),
                   jax.ShapeDtypeStruct(w_up.shape, BF16),
                   jax.ShapeDtypeStruct(w_down.shape, BF16),
                   jax.ShapeDtypeStruct((1, n_mod2), F32),
                   jax.ShapeDtypeStruct((1, n_fmod), F32)),
        grid=(n_steps,),
        in_specs=[
            pl.BlockSpec((ts, D_MODEL), row),
            pl.BlockSpec((1, ts), lambda i: (0, i)),
            _const_spec(mod1.shape),
            _const_spec(g1.shape),
            _const_spec(w_in.shape),
            _const_spec(invf.shape),
            _const_spec(dmask.shape),
            _const_spec(xi.shape),
            _const_spec(zeta.shape),
            _const_spec(rdecay.shape),
            _const_spec(rmask.shape),
            _const_spec(retg.shape),
            _const_spec(scw.shape),
            _const_spec(cng.shape),
            _const_spec(w_out.shape),
            wup_spec,
            wdown_spec,
            pl.BlockSpec((c_rows, 1), row),
            pl.BlockSpec((c_rows, n_mod2), lambda i: (i, 1)),
            pl.BlockSpec((1, n_mod2), lambda i: (0, 1)),
            pl.BlockSpec((c_rows, n_fmod), row),
            pl.BlockSpec((n_fmod,), lambda i: (0,)),
        ],
        out_specs=(pl.BlockSpec((ts, D_MODEL), row), wup_spec, wdown_spec,
                   pl.BlockSpec((1, n_mod2), lambda i: (0, 0)),
                   pl.BlockSpec((1, n_fmod), lambda i: (0, 0))),
        scratch_shapes=[
            pltpu.VMEM((ts, RET_WIDTH), BF16),
            pltpu.VMEM((3, RET_WIDTH, ts), BF16),
            pltpu.VMEM((3, ts, RET_WIDTH), BF16),
            pltpu.VMEM((ts, RET_WIDTH), F32),
            pltpu.VMEM((ts + 2 * SUBLANES, CONV_WIDTH), F32),
            pltpu.VMEM((ts + 2 * SUBLANES, CONV_WIDTH), F32),
            pltpu.VMEM((HEAD_PAIRS, LANES, LANES), F32),
            pltpu.VMEM(w_in.shape, BF16),
            pltpu.VMEM(w_out.shape, BF16),
        ],
        compiler_params=pltpu.CompilerParams(
            dimension_semantics=("arbitrary",), vmem_limit_bytes=VMEM_LIMIT_BYTES),
        name="token_mix",
    )(x, pos_row, mod1, g1, w_in, invf, dmask, xi, zeta, rdecay, rmask, retg, scw, cng, w_out,
      w_up, w_down, c_col, mod_w, mod_b, fmod_w, fmod_b)


def _ffn_kernel(x_ref, mod_ref, fmod_ref, g2_ref, wup_ref, fcw_ref, fcb_ref, wdown_ref, fg_ref,
                o_ref, a1_s, a2_s, act_s):
    ts = x_ref.shape[0]
    _delay_init(a1_s, a2_s, ts, pl.program_id(0) == 0)

    shift2 = mod_ref[:, 0:D_MODEL]
    scale2 = mod_ref[:, D_MODEL:2 * D_MODEL]
    gate2 = mod_ref[:, 2 * D_MODEL:3 * D_MODEL]
    final_shift = fmod_ref[:, 0:D_MODEL]
    final_scale = fmod_ref[:, D_MODEL:2 * D_MODEL]
    x = x_ref[...]
    h = _rms_modulate(x, g2_ref[...], scale2, shift2).astype(BF16)
    for lo in range(0, D_FF, FFN_COL_CHUNK):
        cols = slice(lo, lo + FFN_COL_CHUNK)
        a0 = _wdot(h, wup_ref[:, lo:lo + FFN_COL_CHUNK])
        val = _wdot(h, wup_ref[:, D_FF + lo:D_FF + lo + FFN_COL_CHUNK])
        a = _causal_conv3(a1_s, a2_s, a0, fcw_ref[:, cols], 0, cols) + fcb_ref[:, cols]
        act_s[:, cols] = (_silu(a) * val).astype(BF16)
    r0 = 0
    for nrows in FFN_DOWN_BLOCKS:
        rows = slice(r0, r0 + nrows)
        r0 += nrows
        ffn = _wdot(act_s[rows, :], wdown_ref[...])
        x2 = x[rows] + gate2 * ffn
        o_ref[rows, :] = _rms_modulate(x2, fg_ref[...][None, :], final_scale, final_shift)


def _ffn_call(x, mod, fmod, g2, w_up, fcw, fcb, w_down, fg):
    s = x.shape[0]
    ts = TS_FFN
    row = lambda i: (i, 0)
    return pl.pallas_call(
        _ffn_kernel,
        out_shape=jax.ShapeDtypeStruct((s, D_MODEL), F32),
        grid=(s // ts,),
        in_specs=[
            pl.BlockSpec((ts, D_MODEL), row),
            _const_spec(mod.shape),
            _const_spec(fmod.shape),
            _const_spec(g2.shape),
            _const_spec(w_up.shape),
            _const_spec(fcw.shape),
            _const_spec(fcb.shape),
            _const_spec(w_down.shape),
            _const_spec(fg.shape),
        ],
        out_specs=pl.BlockSpec((ts, D_MODEL), row),
        scratch_shapes=[
            pltpu.VMEM((ts + 2 * SUBLANES, D_FF), F32),
            pltpu.VMEM((ts + 2 * SUBLANES, D_FF), F32),
            pltpu.VMEM((ts, D_FF), BF16),
        ],
        compiler_params=pltpu.CompilerParams(
            dimension_semantics=("arbitrary",), vmem_limit_bytes=VMEM_LIMIT_BYTES),
        name="convglu_ffn",
    )(x, mod, fmod, g2, w_up, fcw, fcb, w_down, fg)


def kernel(x, c, positions, mod_w, mod_b, norm1_g, w_in, ret_norm_g, short_conv_w, conv_norm_g,
           w_out, norm2_g, w_up, ffn_conv_w, ffn_conv_b, w_down, final_mod_w, final_mod_b,
           final_norm_g):
    b, s, d = x.shape
    assert (b, d) == (1, D_MODEL) and mod_w.shape[0] == 1 and s % max(TS_MIX, TS_FFN) == 0
    mod1, c_col = _mod_call(c, mod_w[0], mod_b, 3 * d)

    x2d = x.reshape(s, d)
    x1, w_up_bf, w_down_bf, mod2, fmod = _mix_call(
        x2d, positions, mod1, norm1_g, w_in[0], ret_norm_g, short_conv_w[0], conv_norm_g, w_out[0],
        w_up[0], w_down[0], c_col, mod_w[0], mod_b, final_mod_w, final_mod_b)
    out = _ffn_call(x1, mod2, fmod, norm2_g, w_up_bf, ffn_conv_w[0], ffn_conv_b, w_down_bf,
                    final_norm_g)
    return out.reshape(b, s, d)
```

```python
import numpy as np
import jax
import jax.numpy as jnp
from jax import lax
from jax.experimental import pallas as pl
from jax.experimental.pallas import tpu as pltpu

F32 = jnp.float32
BF16 = jnp.bfloat16

D_MODEL = 1024
RET_WIDTH = 512
RET_HEADS = 8
HEAD_DIM = 64
CONV_WIDTH = 512
GROUP_DIM = 64
CHUNK = 128
D_FF = 2816
ROPE_BASE = 10000.0
EPS = 1e-6
IN_COLS = 4 * RET_WIDTH + 3 * CONV_WIDTH

LANES = 128
SUBLANES = 8
HEAD_PAIRS = RET_WIDTH // LANES
VMEM_LIMIT_BYTES = 56 * 1024 * 1024

TS_MIX = 512
TS_FFN = 512
FFN_COL_CHUNK = 256
FFN_DOWN_BLOCKS = (256, 256)


def _const_spec(shape):
    return pl.BlockSpec(shape, lambda i: (0,) * len(shape), pipeline_mode=pl.Buffered(1))


def _rms_modulate(x, gain, scale, shift):
    ms = jnp.mean(x * x, axis=-1, keepdims=True)
    return x * lax.rsqrt(ms + EPS) * (gain * (1.0 + scale)) + shift


def _silu(x):
    return x * jax.nn.sigmoid(x)


def _wdot(act, w):
    return lax.dot_general(act, w, (((1,), (0,)), ((), ())), preferred_element_type=F32)


MOD_ROWS = 512


def _as_column(row):
    pieces = []
    for lo in range(0, row.shape[1], LANES):
        tile = jnp.broadcast_to(row[:, lo:lo + LANES], (LANES, LANES))
        pieces.append(tile.T[:, 0:1])
    return jnp.concatenate(pieces, axis=0)


def _mod_kernel(c_ref, w_ref, b_ref, o_ref, ccol_ref):
    @pl.when(pl.program_id(0) == 0)
    def _():
        o_ref[...] = b_ref[...]

    c_col = _as_column(c_ref[...])
    ccol_ref[...] = c_col
    o_ref[...] += jnp.sum(_silu(c_col) * w_ref[...], axis=0, keepdims=True)


def _mod_call(c_row, w, b_row, n_cols):
    d = w.shape[0]
    return pl.pallas_call(
        _mod_kernel,
        out_shape=(jax.ShapeDtypeStruct((1, n_cols), F32), jax.ShapeDtypeStruct((d, 1), F32)),
        grid=(d // MOD_ROWS,),
        in_specs=[
            pl.BlockSpec((1, MOD_ROWS), lambda j: (0, j)),
            pl.BlockSpec((MOD_ROWS, n_cols), lambda j: (j, 0)),
            pl.BlockSpec((1, n_cols), lambda j: (0, 0)),
        ],
        out_specs=(pl.BlockSpec((1, n_cols), lambda j: (0, 0)),
                   pl.BlockSpec((MOD_ROWS, 1), lambda j: (j, 0))),
        compiler_params=pltpu.CompilerParams(
            dimension_semantics=("arbitrary",), vmem_limit_bytes=VMEM_LIMIT_BYTES),
        name="adaln_mod",
    )(c_row, w, b_row)


def _group_rms(y, gain):
    lane = lax.broadcasted_iota(jnp.int32, (y.shape[0], LANES), 1)
    group0 = (lane & GROUP_DIM) == 0
    outs = []
    for lo in range(0, y.shape[1], LANES):
        yy = y[:, lo:lo + LANES]
        y2 = yy * yy
        s0 = jnp.sum(jnp.where(group0, y2, 0.0), axis=-1, keepdims=True)
        s1 = jnp.sum(jnp.where(group0, 0.0, y2), axis=-1, keepdims=True)
        ss = jnp.where(group0, s0, s1)
        outs.append(yy * lax.rsqrt(ss * (1.0 / GROUP_DIM) + EPS))
    return jnp.concatenate(outs, axis=1) * gain


def _delay_init(d1_ref, d2_ref, ts, first_step):
    @pl.when(first_step)
    def _():
        _delay_zero(d1_ref, d2_ref, ts)

    _delay_carry(d1_ref, d2_ref, ts)


def _delay_zero(d1_ref, d2_ref, ts):
    past = slice(ts + SUBLANES, ts + 2 * SUBLANES)
    d1_ref[past, :] = jnp.zeros((SUBLANES, d1_ref.shape[1]), F32)
    d2_ref[past, :] = jnp.zeros((SUBLANES, d2_ref.shape[1]), F32)


def _delay_carry(d1_ref, d2_ref, ts):
    head = slice(SUBLANES, 2 * SUBLANES)
    past = slice(ts + SUBLANES, ts + 2 * SUBLANES)
    d1_ref[head, :] = d1_ref[past, :]
    d2_ref[head, :] = d2_ref[past, :]


def _causal_conv3(d1_ref, d2_ref, u, w, row0, cols):
    n = u.shape[0]
    lo = SUBLANES + row0
    d1_ref[lo + 1:lo + n + 1, cols] = u
    d2_ref[lo + 2:lo + n + 2, cols] = u
    return (d2_ref[lo:lo + n, cols] * w[0:1, :] + d1_ref[lo:lo + n, cols] * w[1:2, :]
            + u * w[2:3, :])


def _mix_kernel(x_ref, pos_ref, mod_ref, g1_ref, win_ref, invf_ref, dmask_ref, xi_ref, zeta_ref,
                rdecay_ref, rmask_ref, retg_ref, scw_ref, cng_ref, wout_ref, wup_ref, wdown_ref,
                c_ref, modw_ref, modb_ref, fmodw_ref, fmodb_ref,
                o_ref, wup_bf_ref, wdown_bf_ref, mod2_ref, fmod_ref, q_s, kt_s, v_s, y_s, u1_s, u2_s,
                r_s, win_s, wout_s):
    ts = x_ref.shape[0]
    _delay_init(u1_s, u2_s, ts, pl.program_id(0) == 0)

    @pl.when(pl.program_id(0) == 0)
    def _():
        r_s[...] = jnp.zeros_like(r_s)
        mod2_ref[...] = modb_ref[...]
        fmod_ref[...] = fmodb_ref[...][None, :]
        for col in range(0, IN_COLS, RET_WIDTH):
            win_s[:, col:col + RET_WIDTH] = win_ref[:, col:col + RET_WIDTH].astype(BF16)
        wout_s[...] = wout_ref[...].astype(BF16)

    x = x_ref[...]
    shift1 = mod_ref[:, 0:D_MODEL]
    scale1 = mod_ref[:, D_MODEL:2 * D_MODEL]
    gate1 = mod_ref[:, 2 * D_MODEL:3 * D_MODEL]
    h = _rms_modulate(x, g1_ref[...], scale1, shift1).astype(BF16)

    def proj(col):
        return _wdot(h, win_s[:, col:col + RET_WIDTH])

    half = HEAD_DIM // 2
    ang_t = invf_ref[...] * pos_ref[...].astype(F32)
    cos_t = jnp.cos(ang_t)
    sin_t = jnp.sin(ang_t)
    trig = jnp.concatenate([cos_t, sin_t, cos_t, sin_t], axis=0).T
    lane = lax.broadcasted_iota(jnp.int32, (ts, LANES), 1)
    first_half = (lane & half) == 0
    cos = jnp.where(first_half, trig, pltpu.roll(trig, half, 1))
    sin_signed = jnp.where(first_half, -pltpu.roll(trig, LANES - half, 1), trig)

    def rotary(t):
        swapped = jnp.where(first_half, pltpu.roll(t, LANES - half, 1), pltpu.roll(t, half, 1))
        return t * cos + swapped * sin_signed

    q = proj(0)
    k = proj(RET_WIDTH)
    k_rot = []
    for p in range(HEAD_PAIRS):
        ln = slice(p * LANES, (p + 1) * LANES)
        q_s[:, ln] = rotary(q[:, ln]).astype(BF16)
        k_rot.append(rotary(k[:, ln]) * (HEAD_DIM ** -0.5))
    k_t = jnp.concatenate(k_rot, axis=1).T
    dim_head0 = (lax.broadcasted_iota(jnp.int32, k_t.shape, 0) & HEAD_DIM) == 0
    kt_s[0] = k_t.astype(BF16)
    kt_s[1] = jnp.where(dim_head0, k_t, 0.0).astype(BF16)
    kt_s[2] = jnp.where(dim_head0, 0.0, k_t).astype(BF16)
    v = proj(2 * RET_WIDTH)
    lane_head0 = (lax.broadcasted_iota(jnp.int32, v.shape, 1) & HEAD_DIM) == 0
    v_s[0] = v.astype(BF16)
    v_s[1] = jnp.where(lane_head0, v, 0.0).astype(BF16)
    v_s[2] = jnp.where(lane_head0, 0.0, v).astype(BF16)

    for c in range(ts // CHUNK):
        rows = slice(c * CHUNK, (c + 1) * CHUNK)
        for p in range(HEAD_PAIRS):
            ln = slice(p * LANES, (p + 1) * LANES)
            qc = q_s[rows, ln]
            kcat = jnp.concatenate([kt_s[1, ln, rows], kt_s[2, ln, rows]], axis=1)
            scores = jnp.dot(qc, kcat, preferred_element_type=F32) * dmask_ref[p]
            vcat = jnp.concatenate([v_s[1, rows, ln], v_s[2, rows, ln]], axis=0)
            y_inner = jnp.dot(scores.astype(BF16), vcat, preferred_element_type=F32)
            r = r_s[p]
            y_cross = jnp.dot(qc, r.astype(BF16), preferred_element_type=F32) * xi_ref[p]
            y_s[rows, ln] = y_inner + y_cross
            vz = (v_s[0, rows, ln].astype(F32) * zeta_ref[p]).astype(BF16)
            kv = jnp.dot(kt_s[0, ln, rows], vz, preferred_element_type=F32)
            r_s[p] = r * rdecay_ref[p] + kv * rmask_ref[p]

    y_ret = _group_rms(y_s[...], retg_ref[...]) * _silu(proj(3 * RET_WIDTH))

    u = proj(4 * RET_WIDTH + CONV_WIDTH) * proj(4 * RET_WIDTH + 2 * CONV_WIDTH)
    y_conv = proj(4 * RET_WIDTH) * _causal_conv3(u1_s, u2_s, u, scw_ref[...], 0, slice(None))
    y_conv = _group_rms(y_conv, cng_ref[...])

    cat = jnp.concatenate([y_ret.astype(BF16), y_conv.astype(BF16)], axis=1)
    mixed = _wdot(cat, wout_s[...])
    o_ref[...] = x + gate1 * mixed

    wup_bf_ref[...] = wup_ref[...].astype(BF16)
    wdown_bf_ref[...] = wdown_ref[...].astype(BF16)
    ca = _silu(c_ref[...])
    mod2_ref[...] += jnp.sum(ca * modw_ref[...], axis=0, keepdims=True)
    fmod_ref[...] += jnp.sum(ca * fmodw_ref[...], axis=0, keepdims=True)


def _retention_tables():
    h = np.arange(RET_HEADS, dtype=np.float64)
    log_gamma = np.log1p(-np.exp2(-5.0 - h))
    idx = np.arange(CHUNK, dtype=np.float64)
    rel = idx[:, None] - idx[None, :]
    dmask = np.where(rel[None] >= 0.0,
                     np.exp(np.maximum(rel, 0.0)[None] * log_gamma[:, None, None]), 0.0)
    dmask = dmask.reshape(HEAD_PAIRS, 2, CHUNK, CHUNK).transpose(0, 2, 1, 3)
    dmask = dmask.reshape(HEAD_PAIRS, CHUNK, 2 * CHUNK)
    zeta = np.exp((CHUNK - 1.0 - idx)[None, :] * log_gamma[:, None])
    xi = np.exp((idx + 1.0)[None, :] * log_gamma[:, None])
    chunk_decay = np.exp(CHUNK * log_gamma)

    def per_lane(t):
        t = t.reshape(HEAD_PAIRS, 2, CHUNK)
        return np.repeat(np.transpose(t, (0, 2, 1)), HEAD_DIM, axis=2)

    lane_head = np.arange(LANES) // HEAD_DIM
    rmask = np.broadcast_to(lane_head[:, None] == lane_head[None, :], (HEAD_PAIRS, LANES, LANES))
    rdecay = np.repeat(chunk_decay.reshape(HEAD_PAIRS, 1, 2), HEAD_DIM, axis=2)
    rdecay = np.broadcast_to(rdecay, (HEAD_PAIRS, LANES, LANES))
    return tuple(jnp.asarray(np.ascontiguousarray(t), dtype=F32)
                 for t in (dmask, per_lane(xi), per_lane(zeta), rdecay, rmask))


def _side_cast_spec(w, n_steps):
    k, n = w.shape
    nblk = next(b for b in range(n_steps, 0, -1)
                if k % b == 0 and (k // b) % (2 * SUBLANES) == 0)
    return pl.BlockSpec((k // nblk, n), lambda i: (jnp.minimum(i, nblk - 1), 0))


def _mix_call(x, pos_row, mod1, g1, w_in, retg, scw, cng, w_out, w_up, w_down,
              c_col, mod_w, mod_b, fmod_w, fmod_b):
    s = x.shape[0]
    ts = TS_MIX
    n_steps = s // ts
    half = HEAD_DIM // 2
    invf = (ROPE_BASE ** (-jnp.arange(half, dtype=F32) / half))[:, None]
    dmask, xi, zeta, rdecay, rmask = _retention_tables()
    row = lambda i: (i, 0)
    wup_spec = _side_cast_spec(w_up, n_steps)
    wdown_spec = _side_cast_spec(w_down, n_steps)
    n_mod2 = mod1.shape[1]
    assert mod_w.shape[1] == 2 * n_mod2
    n_fmod = fmod_w.shape[1]
    c_rows = c_col.shape[0] // n_steps
    return pl.pallas_call(
        _mix_kernel,
        out_shape=(jax.ShapeDtypeStruct((s, D_MODEL), F32),
                   jax.ShapeDtypeStruct(w_up.shape, BF16),
                   jax.ShapeDtypeStruct(w_down.shape, BF16),
                   jax.ShapeDtypeStruct((1, n_mod2), F32),
                   jax.ShapeDtypeStruct((1, n_fmod), F32)),
        grid=(n_steps,),
        in_specs=[
            pl.BlockSpec((ts, D_MODEL), row),
            pl.BlockSpec((1, ts), lambda i: (0, i)),
            _const_spec(mod1.shape),
            _const_spec(g1.shape),
            _const_spec(w_in.shape),
            _const_spec(invf.shape),
            _const_spec(dmask.shape),
            _const_spec(xi.shape),
            _const_spec(zeta.shape),
            _const_spec(rdecay.shape),
            _const_spec(rmask.shape),
            _const_spec(retg.shape),
            _const_spec(scw.shape),
            _const_spec(cng.shape),
            _const_spec(w_out.shape),
            wup_spec,
            wdown_spec,
            pl.BlockSpec((c_rows, 1), row),
            pl.BlockSpec((c_rows, n_mod2), lambda i: (i, 1)),
            pl.BlockSpec((1, n_mod2), lambda i: (0, 1)),
            pl.BlockSpec((c_rows, n_fmod), row),
            pl.BlockSpec((n_fmod,), lambda i: (0,)),
        ],
        out_specs=(pl.BlockSpec((ts, D_MODEL), row), wup_spec, wdown_spec,
                   pl.BlockSpec((1, n_mod2), lambda i: (0, 0)),
                   pl.BlockSpec((1, n_fmod), lambda i: (0, 0))),
        scratch_shapes=[
            pltpu.VMEM((ts, RET_WIDTH), BF16),
            pltpu.VMEM((3, RET_WIDTH, ts), BF16),
            pltpu.VMEM((3, ts, RET_WIDTH), BF16),
            pltpu.VMEM((ts, RET_WIDTH), F32),
            pltpu.VMEM((ts + 2 * SUBLANES, CONV_WIDTH), F32),
            pltpu.VMEM((ts + 2 * SUBLANES, CONV_WIDTH), F32),
            pltpu.VMEM((HEAD_PAIRS, LANES, LANES), F32),
            pltpu.VMEM(w_in.shape, BF16),
            pltpu.VMEM(w_out.shape, BF16),
        ],
        compiler_params=pltpu.CompilerParams(
            dimension_semantics=("arbitrary",), vmem_limit_bytes=VMEM_LIMIT_BYTES),
        name="token_mix",
    )(x, pos_row, mod1, g1, w_in, invf, dmask, xi, zeta, rdecay, rmask, retg, scw, cng, w_out,
      w_up, w_down, c_col, mod_w, mod_b, fmod_w, fmod_b)


def _ffn_kernel(x_hbm, mod_ref, fmod_ref, g2_ref, wup_ref, fcw_ref, fcb_ref, wdown_ref, fg_ref,
                o_hbm, a1_s, a2_s, act_s):
    ts = TS_FFN
    _delay_zero(a1_s, a2_s, ts)

    def tile(x_ref, o_ref):
        _ffn_tile(x_ref, mod_ref, fmod_ref, g2_ref, wup_ref, fcw_ref, fcb_ref, wdown_ref, fg_ref,
                  o_ref, a1_s, a2_s, act_s)

    row = lambda i: (i, 0)
    pltpu.emit_pipeline(
        tile, grid=(x_hbm.shape[0] // ts,),
        in_specs=[pl.BlockSpec((ts, D_MODEL), row)],
        out_specs=[pl.BlockSpec((ts, D_MODEL), row)],
    )(x_hbm, o_hbm)


def _ffn_tile(x_ref, mod_ref, fmod_ref, g2_ref, wup_ref, fcw_ref, fcb_ref, wdown_ref, fg_ref,
              o_ref, a1_s, a2_s, act_s):
    ts = x_ref.shape[0]
    _delay_carry(a1_s, a2_s, ts)

    shift2 = mod_ref[:, 0:D_MODEL]
    scale2 = mod_ref[:, D_MODEL:2 * D_MODEL]
    gate2 = mod_ref[:, 2 * D_MODEL:3 * D_MODEL]
    final_shift = fmod_ref[:, 0:D_MODEL]
    final_scale = fmod_ref[:, D_MODEL:2 * D_MODEL]
    x = x_ref[...]
    h = _rms_modulate(x, g2_ref[...], scale2, shift2).astype(BF16)
    for lo in range(0, D_FF, FFN_COL_CHUNK):
        cols = slice(lo, lo + FFN_COL_CHUNK)
        a0 = _wdot(h, wup_ref[:, lo:lo + FFN_COL_CHUNK])
        val = _wdot(h, wup_ref[:, D_FF + lo:D_FF + lo + FFN_COL_CHUNK])
        a = _causal_conv3(a1_s, a2_s, a0, fcw_ref[:, cols], 0, cols) + fcb_ref[:, cols]
        act_s[:, cols] = (_silu(a) * val).astype(BF16)
    r0 = 0
    for nrows in FFN_DOWN_BLOCKS:
        rows = slice(r0, r0 + nrows)
        r0 += nrows
        ffn = _wdot(act_s[rows, :], wdown_ref[...])
        x2 = x[rows] + gate2 * ffn
        o_ref[rows, :] = _rms_modulate(x2, fg_ref[...][None, :], final_scale, final_shift)


def _ffn_call(x, mod, fmod, g2, w_up, fcw, fcb, w_down, fg):
    s = x.shape[0]
    ts = TS_FFN
    vmem = pl.BlockSpec(memory_space=pltpu.VMEM)
    return pl.pallas_call(
        _ffn_kernel,
        out_shape=jax.ShapeDtypeStruct((s, D_MODEL), F32),
        in_specs=[pl.BlockSpec(memory_space=pl.ANY)] + [vmem] * 8,
        out_specs=pl.BlockSpec(memory_space=pl.ANY),
        scratch_shapes=[
            pltpu.VMEM((ts + 2 * SUBLANES, D_FF), F32),
            pltpu.VMEM((ts + 2 * SUBLANES, D_FF), F32),
            pltpu.VMEM((ts, D_FF), BF16),
        ],
        compiler_params=pltpu.CompilerParams(vmem_limit_bytes=VMEM_LIMIT_BYTES),
        name="convglu_ffn",
    )(x, mod, fmod, g2, w_up, fcw, fcb, w_down, fg)


def kernel(x, c, positions, mod_w, mod_b, norm1_g, w_in, ret_norm_g, short_conv_w, conv_norm_g,
           w_out, norm2_g, w_up, ffn_conv_w, ffn_conv_b, w_down, final_mod_w, final_mod_b,
           final_norm_g):
    b, s, d = x.shape
    assert (b, d) == (1, D_MODEL) and mod_w.shape[0] == 1 and s % max(TS_MIX, TS_FFN) == 0
    mod1, c_col = _mod_call(c, mod_w[0], mod_b, 3 * d)

    x2d = x.reshape(s, d)
    x1, w_up_bf, w_down_bf, mod2, fmod = _mix_call(
        x2d, positions, mod1, norm1_g, w_in[0], ret_norm_g, short_conv_w[0], conv_norm_g, w_out[0],
        w_up[0], w_down[0], c_col, mod_w[0], mod_b, final_mod_w, final_mod_b)
    out = _ffn_call(x1, mod2, fmod, norm2_g, w_up_bf, ffn_conv_w[0], ffn_conv_b, w_down_bf,
                    final_norm_g)
    return out.reshape(b, s, d)
```

```python
import functools

import numpy as np
import jax
import jax.numpy as jnp
from jax import lax
from jax.experimental import pallas as pl
from jax.experimental.pallas import tpu as pltpu

F32 = jnp.float32
BF16 = jnp.bfloat16

D_MODEL = 1024
RET_WIDTH = 512
RET_HEADS = 8
HEAD_DIM = 64
CONV_WIDTH = 512
GROUP_DIM = 64
CHUNK = 128
D_FF = 2816
ROPE_BASE = 10000.0
EPS = 1e-6
IN_COLS = 4 * RET_WIDTH + 3 * CONV_WIDTH

LANES = 128
SUBLANES = 8
HEAD_PAIRS = RET_WIDTH // LANES
VMEM_LIMIT_BYTES = 56 * 1024 * 1024

TS_MIX = 512
TS_FFN = 512
FFN_COL_CHUNK = 256
FFN_DOWN_BLOCKS = (256, 256)


def _const_spec(shape):
    return pl.BlockSpec(shape, lambda i: (0,) * len(shape), pipeline_mode=pl.Buffered(1))


def _rms_modulate(x, gain, scale, shift):
    ms = jnp.mean(x * x, axis=-1, keepdims=True)
    return x * lax.rsqrt(ms + EPS) * (gain * (1.0 + scale)) + shift


def _silu(x):
    return x * jax.nn.sigmoid(x)


def _wdot(act, w):
    return lax.dot_general(act, w, (((1,), (0,)), ((), ())), preferred_element_type=F32)


MOD_ROWS = 512


def _as_column(row):
    pieces = []
    for lo in range(0, row.shape[1], LANES):
        tile = jnp.broadcast_to(row[:, lo:lo + LANES], (LANES, LANES))
        pieces.append(tile.T[:, 0:1])
    return jnp.concatenate(pieces, axis=0)


def _mod_kernel(c_ref, w_ref, b_ref, o_ref, ccol_ref):
    @pl.when(pl.program_id(0) == 0)
    def _():
        o_ref[...] = b_ref[...]

    c_col = _as_column(c_ref[...])
    ccol_ref[...] = c_col
    o_ref[...] += jnp.sum(_silu(c_col) * w_ref[...], axis=0, keepdims=True)


def _mod_call(c_row, w, b_row, n_cols):
    d = w.shape[0]
    return pl.pallas_call(
        _mod_kernel,
        out_shape=(jax.ShapeDtypeStruct((1, n_cols), F32), jax.ShapeDtypeStruct((d, 1), F32)),
        grid=(d // MOD_ROWS,),
        in_specs=[
            pl.BlockSpec((1, MOD_ROWS), lambda j: (0, j)),
            pl.BlockSpec((MOD_ROWS, n_cols), lambda j: (j, 0)),
            pl.BlockSpec((1, n_cols), lambda j: (0, 0)),
        ],
        out_specs=(pl.BlockSpec((1, n_cols), lambda j: (0, 0)),
                   pl.BlockSpec((MOD_ROWS, 1), lambda j: (j, 0))),
        compiler_params=pltpu.CompilerParams(
            dimension_semantics=("arbitrary",), vmem_limit_bytes=VMEM_LIMIT_BYTES),
        name="adaln_mod",
    )(c_row, w, b_row)


def _group_rms(y, gain):
    lane = lax.broadcasted_iota(jnp.int32, (y.shape[0], LANES), 1)
    group0 = (lane & GROUP_DIM) == 0
    outs = []
    for lo in range(0, y.shape[1], LANES):
        yy = y[:, lo:lo + LANES]
        y2 = yy * yy
        s0 = jnp.sum(jnp.where(group0, y2, 0.0), axis=-1, keepdims=True)
        s1 = jnp.sum(jnp.where(group0, 0.0, y2), axis=-1, keepdims=True)
        ss = jnp.where(group0, s0, s1)
        outs.append(yy * lax.rsqrt(ss * (1.0 / GROUP_DIM) + EPS))
    return jnp.concatenate(outs, axis=1) * gain


def _delay_zero(d1_ref, d2_ref, ts):
    past = slice(ts + SUBLANES, ts + 2 * SUBLANES)
    d1_ref[past, :] = jnp.zeros((SUBLANES, d1_ref.shape[1]), F32)
    d2_ref[past, :] = jnp.zeros((SUBLANES, d2_ref.shape[1]), F32)


def _delay_carry(d1_ref, d2_ref, ts):
    head = slice(SUBLANES, 2 * SUBLANES)
    past = slice(ts + SUBLANES, ts + 2 * SUBLANES)
    d1_ref[head, :] = d1_ref[past, :]
    d2_ref[head, :] = d2_ref[past, :]


def _causal_conv3(d1_ref, d2_ref, u, w, row0, cols):
    n = u.shape[0]
    lo = SUBLANES + row0
    d1_ref[lo + 1:lo + n + 1, cols] = u
    d2_ref[lo + 2:lo + n + 2, cols] = u
    return (d2_ref[lo:lo + n, cols] * w[0:1, :] + d1_ref[lo:lo + n, cols] * w[1:2, :]
            + u * w[2:3, :])


def _mix_kernel(x_hbm, pos_hbm, mod_ref, g1_ref, win_ref, invf_ref, dmask_ref, xi_ref, zeta_ref,
                rdecay_ref, rmask_ref, retg_ref, scw_ref, cng_ref, wout_ref, wup_hbm, wdown_hbm,
                c_hbm, modw_hbm, modb_ref, fmodw_hbm, fmodb_ref,
                o_hbm, wup_bf_hbm, wdown_bf_hbm, mod2_ref, fmod_ref, q_s, kt_s, v_s, y_s, u1_s, u2_s,
                r_s, win_s, wout_s, *, side_specs):
    ts = TS_MIX
    n_mod2 = mod2_ref.shape[1]
    _delay_zero(u1_s, u2_s, ts)
    r_s[...] = jnp.zeros_like(r_s)
    mod2_ref[...] = modb_ref[:, n_mod2:2 * n_mod2]
    fmod_ref[...] = fmodb_ref[...][None, :]
    for col in range(0, IN_COLS, RET_WIDTH):
        win_s[:, col:col + RET_WIDTH] = win_ref[:, col:col + RET_WIDTH].astype(BF16)
    wout_s[...] = wout_ref[...].astype(BF16)

    tile = functools.partial(
        _mix_tile, mod_ref=mod_ref, g1_ref=g1_ref, invf_ref=invf_ref, dmask_ref=dmask_ref,
        xi_ref=xi_ref, zeta_ref=zeta_ref, rdecay_ref=rdecay_ref, rmask_ref=rmask_ref,
        retg_ref=retg_ref, scw_ref=scw_ref, cng_ref=cng_ref, mod2_ref=mod2_ref, fmod_ref=fmod_ref,
        q_s=q_s, kt_s=kt_s, v_s=v_s, y_s=y_s, u1_s=u1_s, u2_s=u2_s, r_s=r_s, win_s=win_s,
        wout_s=wout_s)
    in_specs, out_specs = side_specs
    pltpu.emit_pipeline(
        tile, grid=(x_hbm.shape[0] // ts,), in_specs=in_specs, out_specs=out_specs,
    )(x_hbm, pos_hbm, wup_hbm, wdown_hbm, c_hbm, modw_hbm, fmodw_hbm,
      o_hbm, wup_bf_hbm, wdown_bf_hbm)


def _mix_tile(x_ref, pos_ref, wup_ref, wdown_ref, c_ref, modw_ref, fmodw_ref,
              o_ref, wup_bf_ref, wdown_bf_ref, *, mod_ref, g1_ref, invf_ref, dmask_ref, xi_ref,
              zeta_ref, rdecay_ref, rmask_ref, retg_ref, scw_ref, cng_ref, mod2_ref, fmod_ref,
              q_s, kt_s, v_s, y_s, u1_s, u2_s, r_s, win_s, wout_s):
    ts = x_ref.shape[0]
    _delay_carry(u1_s, u2_s, ts)

    x = x_ref[...]
    shift1 = mod_ref[:, 0:D_MODEL]
    scale1 = mod_ref[:, D_MODEL:2 * D_MODEL]
    gate1 = mod_ref[:, 2 * D_MODEL:3 * D_MODEL]
    h = _rms_modulate(x, g1_ref[...], scale1, shift1).astype(BF16)

    def proj(col):
        return _wdot(h, win_s[:, col:col + RET_WIDTH])

    half = HEAD_DIM // 2
    ang_t = invf_ref[...] * pos_ref[...].astype(F32)
    cos_t = jnp.cos(ang_t)
    sin_t = jnp.sin(ang_t)
    trig = jnp.concatenate([cos_t, sin_t, cos_t, sin_t], axis=0).T
    lane = lax.broadcasted_iota(jnp.int32, (ts, LANES), 1)
    first_half = (lane & half) == 0
    cos = jnp.where(first_half, trig, pltpu.roll(trig, half, 1))
    sin_signed = jnp.where(first_half, -pltpu.roll(trig, LANES - half, 1), trig)

    def rotary(t):
        swapped = jnp.where(first_half, pltpu.roll(t, LANES - half, 1), pltpu.roll(t, half, 1))
        return t * cos + swapped * sin_signed

    q = proj(0)
    k = proj(RET_WIDTH)
    k_rot = []
    for p in range(HEAD_PAIRS):
        ln = slice(p * LANES, (p + 1) * LANES)
        q_s[:, ln] = rotary(q[:, ln]).astype(BF16)
        k_rot.append(rotary(k[:, ln]) * (HEAD_DIM ** -0.5))
    k_t = jnp.concatenate(k_rot, axis=1).T
    dim_head0 = (lax.broadcasted_iota(jnp.int32, k_t.shape, 0) & HEAD_DIM) == 0
    kt_s[0] = k_t.astype(BF16)
    kt_s[1] = jnp.where(dim_head0, k_t, 0.0).astype(BF16)
    kt_s[2] = jnp.where(dim_head0, 0.0, k_t).astype(BF16)
    v = proj(2 * RET_WIDTH)
    lane_head0 = (lax.broadcasted_iota(jnp.int32, v.shape, 1) & HEAD_DIM) == 0
    v_s[0] = v.astype(BF16)
    v_s[1] = jnp.where(lane_head0, v, 0.0).astype(BF16)
    v_s[2] = jnp.where(lane_head0, 0.0, v).astype(BF16)

    for c in range(ts // CHUNK):
        rows = slice(c * CHUNK, (c + 1) * CHUNK)
        for p in range(HEAD_PAIRS):
            ln = slice(p * LANES, (p + 1) * LANES)
            qc = q_s[rows, ln]
            kcat = jnp.concatenate([kt_s[1, ln, rows], kt_s[2, ln, rows]], axis=1)
            scores = jnp.dot(qc, kcat, preferred_element_type=F32) * dmask_ref[p]
            vcat = jnp.concatenate([v_s[1, rows, ln], v_s[2, rows, ln]], axis=0)
            y_inner = jnp.dot(scores.astype(BF16), vcat, preferred_element_type=F32)
            r = r_s[p]
            y_cross = jnp.dot(qc, r.astype(BF16), preferred_element_type=F32) * xi_ref[p]
            y_s[rows, ln] = y_inner + y_cross
            vz = (v_s[0, rows, ln].astype(F32) * zeta_ref[p]).astype(BF16)
            kv = jnp.dot(kt_s[0, ln, rows], vz, preferred_element_type=F32)
            r_s[p] = r * rdecay_ref[p] + kv * rmask_ref[p]

    y_ret = _group_rms(y_s[...], retg_ref[...]) * _silu(proj(3 * RET_WIDTH))

    u = proj(4 * RET_WIDTH + CONV_WIDTH) * proj(4 * RET_WIDTH + 2 * CONV_WIDTH)
    y_conv = proj(4 * RET_WIDTH) * _causal_conv3(u1_s, u2_s, u, scw_ref[...], 0, slice(None))
    y_conv = _group_rms(y_conv, cng_ref[...])

    cat = jnp.concatenate([y_ret.astype(BF16), y_conv.astype(BF16)], axis=1)
    mixed = _wdot(cat, wout_s[...])
    o_ref[...] = x + gate1 * mixed

    wup_bf_ref[...] = wup_ref[...].astype(BF16)
    wdown_bf_ref[...] = wdown_ref[...].astype(BF16)
    ca = _silu(c_ref[...])
    mod2_ref[...] += jnp.sum(ca * modw_ref[...], axis=0, keepdims=True)
    fmod_ref[...] += jnp.sum(ca * fmodw_ref[...], axis=0, keepdims=True)


def _retention_tables():
    h = np.arange(RET_HEADS, dtype=np.float64)
    log_gamma = np.log1p(-np.exp2(-5.0 - h))
    idx = np.arange(CHUNK, dtype=np.float64)
    rel = idx[:, None] - idx[None, :]
    dmask = np.where(rel[None] >= 0.0,
                     np.exp(np.maximum(rel, 0.0)[None] * log_gamma[:, None, None]), 0.0)
    dmask = dmask.reshape(HEAD_PAIRS, 2, CHUNK, CHUNK).transpose(0, 2, 1, 3)
    dmask = dmask.reshape(HEAD_PAIRS, CHUNK, 2 * CHUNK)
    zeta = np.exp((CHUNK - 1.0 - idx)[None, :] * log_gamma[:, None])
    xi = np.exp((idx + 1.0)[None, :] * log_gamma[:, None])
    chunk_decay = np.exp(CHUNK * log_gamma)

    def per_lane(t):
        t = t.reshape(HEAD_PAIRS, 2, CHUNK)
        return np.repeat(np.transpose(t, (0, 2, 1)), HEAD_DIM, axis=2)

    lane_head = np.arange(LANES) // HEAD_DIM
    rmask = np.broadcast_to(lane_head[:, None] == lane_head[None, :], (HEAD_PAIRS, LANES, LANES))
    rdecay = np.repeat(chunk_decay.reshape(HEAD_PAIRS, 1, 2), HEAD_DIM, axis=2)
    rdecay = np.broadcast_to(rdecay, (HEAD_PAIRS, LANES, LANES))
    return tuple(jnp.asarray(np.ascontiguousarray(t), dtype=F32)
                 for t in (dmask, per_lane(xi), per_lane(zeta), rdecay, rmask))


def _side_cast_spec(w, n_steps):
    k, n = w.shape
    nblk = next(b for b in range(n_steps, 0, -1)
                if k % b == 0 and (k // b) % (2 * SUBLANES) == 0)
    return pl.BlockSpec((k // nblk, n), lambda i: (jnp.minimum(i, nblk - 1), 0))


def _mix_call(x, pos_row, mod1, g1, w_in, retg, scw, cng, w_out, w_up, w_down,
              c_col, mod_w, mod_b, fmod_w, fmod_b):
    s = x.shape[0]
    ts = TS_MIX
    n_steps = s // ts
    half = HEAD_DIM // 2
    invf = (ROPE_BASE ** (-jnp.arange(half, dtype=F32) / half))[:, None]
    dmask, xi, zeta, rdecay, rmask = _retention_tables()
    row = lambda i: (i, 0)
    wup_spec = _side_cast_spec(w_up, n_steps)
    wdown_spec = _side_cast_spec(w_down, n_steps)
    n_mod2 = mod1.shape[1]
    assert mod_w.shape[1] == 2 * n_mod2
    n_fmod = fmod_w.shape[1]
    c_rows = c_col.shape[0] // n_steps
    side_specs = (
        [pl.BlockSpec((ts, D_MODEL), row), pl.BlockSpec((1, ts), lambda i: (0, i)), wup_spec,
         wdown_spec, pl.BlockSpec((c_rows, 1), row),
         pl.BlockSpec((c_rows, n_mod2), lambda i: (i, 1)), pl.BlockSpec((c_rows, n_fmod), row)],
        [pl.BlockSpec((ts, D_MODEL), row), wup_spec, wdown_spec])
    hbm = pl.BlockSpec(memory_space=pl.ANY)
    vmem = pl.BlockSpec(memory_space=pltpu.VMEM)
    return pl.pallas_call(
        functools.partial(_mix_kernel, side_specs=side_specs),
        out_shape=(jax.ShapeDtypeStruct((s, D_MODEL), F32),
                   jax.ShapeDtypeStruct(w_up.shape, BF16),
                   jax.ShapeDtypeStruct(w_down.shape, BF16),
                   jax.ShapeDtypeStruct((1, n_mod2), F32),
                   jax.ShapeDtypeStruct((1, n_fmod), F32)),
        in_specs=[hbm, hbm] + [vmem] * 13 + [hbm, hbm, hbm, hbm, vmem, hbm, vmem],
        out_specs=(hbm, hbm, hbm, vmem, vmem),
        scratch_shapes=[
            pltpu.VMEM((ts, RET_WIDTH), BF16),
            pltpu.VMEM((3, RET_WIDTH, ts), BF16),
            pltpu.VMEM((3, ts, RET_WIDTH), BF16),
            pltpu.VMEM((ts, RET_WIDTH), F32),
            pltpu.VMEM((ts + 2 * SUBLANES, CONV_WIDTH), F32),
            pltpu.VMEM((ts + 2 * SUBLANES, CONV_WIDTH), F32),
            pltpu.VMEM((HEAD_PAIRS, LANES, LANES), F32),
            pltpu.VMEM(w_in.shape, BF16),
            pltpu.VMEM(w_out.shape, BF16),
        ],
        compiler_params=pltpu.CompilerParams(vmem_limit_bytes=VMEM_LIMIT_BYTES),
        name="token_mix",
    )(x, pos_row, mod1, g1, w_in, invf, dmask, xi, zeta, rdecay, rmask, retg, scw, cng, w_out,
      w_up, w_down, c_col, mod_w, mod_b, fmod_w, fmod_b)


def _ffn_kernel(x_hbm, mod_ref, fmod_ref, g2_ref, wup_ref, fcw_ref, fcb_ref, wdown_ref, fg_ref,
                o_hbm, a1_s, a2_s, act_s):
    ts = TS_FFN
    _delay_zero(a1_s, a2_s, ts)

    def tile(x_ref, o_ref):
        _ffn_tile(x_ref, mod_ref, fmod_ref, g2_ref, wup_ref, fcw_ref, fcb_ref, wdown_ref, fg_ref,
                  o_ref, a1_s, a2_s, act_s)

    row = lambda i: (i, 0)
    pltpu.emit_pipeline(
        tile, grid=(x_hbm.shape[0] // ts,),
        in_specs=[pl.BlockSpec((ts, D_MODEL), row)],
        out_specs=[pl.BlockSpec((ts, D_MODEL), row)],
    )(x_hbm, o_hbm)


def _ffn_tile(x_ref, mod_ref, fmod_ref, g2_ref, wup_ref, fcw_ref, fcb_ref, wdown_ref, fg_ref,
              o_ref, a1_s, a2_s, act_s):
    ts = x_ref.shape[0]
    _delay_carry(a1_s, a2_s, ts)

    shift2 = mod_ref[:, 0:D_MODEL]
    scale2 = mod_ref[:, D_MODEL:2 * D_MODEL]
    gate2 = mod_ref[:, 2 * D_MODEL:3 * D_MODEL]
    final_shift = fmod_ref[:, 0:D_MODEL]
    final_scale = fmod_ref[:, D_MODEL:2 * D_MODEL]
    x = x_ref[...]
    h = _rms_modulate(x, g2_ref[...], scale2, shift2).astype(BF16)
    for lo in range(0, D_FF, FFN_COL_CHUNK):
        cols = slice(lo, lo + FFN_COL_CHUNK)
        a0 = _wdot(h, wup_ref[:, lo:lo + FFN_COL_CHUNK])
        val = _wdot(h, wup_ref[:, D_FF + lo:D_FF + lo + FFN_COL_CHUNK])
        a = _causal_conv3(a1_s, a2_s, a0, fcw_ref[:, cols], 0, cols) + fcb_ref[:, cols]
        act_s[:, cols] = (_silu(a) * val).astype(BF16)
    r0 = 0
    for nrows in FFN_DOWN_BLOCKS:
        rows = slice(r0, r0 + nrows)
        r0 += nrows
        ffn = _wdot(act_s[rows, :], wdown_ref[...])
        x2 = x[rows] + gate2 * ffn
        o_ref[rows, :] = _rms_modulate(x2, fg_ref[...][None, :], final_scale, final_shift)


def _ffn_call(x, mod, fmod, g2, w_up, fcw, fcb, w_down, fg):
    s = x.shape[0]
    ts = TS_FFN
    vmem = pl.BlockSpec(memory_space=pltpu.VMEM)
    return pl.pallas_call(
        _ffn_kernel,
        out_shape=jax.ShapeDtypeStruct((s, D_MODEL), F32),
        in_specs=[pl.BlockSpec(memory_space=pl.ANY)] + [vmem] * 8,
        out_specs=pl.BlockSpec(memory_space=pl.ANY),
        scratch_shapes=[
            pltpu.VMEM((ts + 2 * SUBLANES, D_FF), F32),
            pltpu.VMEM((ts + 2 * SUBLANES, D_FF), F32),
            pltpu.VMEM((ts, D_FF), BF16),
        ],
        compiler_params=pltpu.CompilerParams(vmem_limit_bytes=VMEM_LIMIT_BYTES),
        name="convglu_ffn",
    )(x, mod, fmod, g2, w_up, fcw, fcb, w_down, fg)


def kernel(x, c, positions, mod_w, mod_b, norm1_g, w_in, ret_norm_g, short_conv_w, conv_norm_g,
           w_out, norm2_g, w_up, ffn_conv_w, ffn_conv_b, w_down, final_mod_w, final_mod_b,
           final_norm_g):
    b, s, d = x.shape
    assert (b, d) == (1, D_MODEL) and mod_w.shape[0] == 1 and s % max(TS_MIX, TS_FFN) == 0
    mod1, c_col = _mod_call(c, mod_w[0], mod_b, 3 * d)

    x2d = x.reshape(s, d)
    x1, w_up_bf, w_down_bf, mod2, fmod = _mix_call(
        x2d, positions, mod1, norm1_g, w_in[0], ret_norm_g, short_conv_w[0], conv_norm_g, w_out[0],
        w_up[0], w_down[0], c_col, mod_w[0], mod_b, final_mod_w, final_mod_b)
    out = _ffn_call(x1, mod2, fmod, norm2_g, w_up_bf, ffn_conv_w[0], ffn_conv_b, w_down_bf,
                    final_norm_g)
    return out.reshape(b, s, d)
```
